```python
import jax, jax.numpy as jnp
from jax import lax
import numpy as np

D_MODEL = 2048
BATCH = 2
SEQ = 16384
DEPTH = 2

D_MIX = D_MODEL
HEAD = 128
POOL_WINDOWS = (2, 4, 8, 16)
POOL_GROUP = HEAD
POOL_WIDTH = POOL_GROUP * len(POOL_WINDOWS)
SC_WIDTH = 3 * D_MIX // 8
CF_WIDTH = D_MIX - POOL_WIDTH - SC_WIDTH
SC_KERNEL = 3
CF_KERNEL = 31
IN_SPLITS = (POOL_WIDTH,
             POOL_WIDTH + SC_WIDTH,
             POOL_WIDTH + 2 * SC_WIDTH,
             POOL_WIDTH + 3 * SC_WIDTH,
             POOL_WIDTH + 3 * SC_WIDTH + CF_WIDTH)
D_IN_PROJ = POOL_WIDTH + 3 * SC_WIDTH + 2 * CF_WIDTH
D_FF = 5632
N_EXPERTS = 8
TOP_K = 2
D_FF_EXPERT = 7168
N_DENSE = (DEPTH + 1) // 2
N_MOE = DEPTH // 2
N_MOD = 6
EPS = 1e-6

kernel_name = "hybrid_pool_shortconv_conformer_moe_block"


def rms_norm(x, g):
    xf = x.astype(jnp.float32)
    y = xf * lax.rsqrt(jnp.mean(xf * xf, axis=-1, keepdims=True) + EPS)
    return (y * g.astype(jnp.float32)).astype(x.dtype)


def layer_norm(x, g, b):
    xf = x.astype(jnp.float32)
    mu = jnp.mean(xf, axis=-1, keepdims=True)
    var = jnp.mean(jnp.square(xf - mu), axis=-1, keepdims=True)
    y = (xf - mu) * lax.rsqrt(var + EPS)
    return (y * g.astype(jnp.float32) + b.astype(jnp.float32)).astype(x.dtype)


def causal_depthwise_conv(u, w):
    K, C = w.shape
    return lax.conv_general_dilated(
        u, w[:, None, :].astype(u.dtype), window_strides=(1,), padding=[(K - 1, 0)],
        dimension_numbers=("NWC", "WIO", "NWC"), feature_group_count=C)


def pool_mixer(v, pool_w, pool_scale):
    B, S, _ = v.shape
    vf = v.astype(jnp.float32)
    cs = jnp.cumsum(vf, axis=1)
    t1 = jnp.arange(1, S + 1, dtype=jnp.float32)[:, None]
    outs = []
    for g, w in enumerate(POOL_WINDOWS):
        sl = slice(g * POOL_GROUP, (g + 1) * POOL_GROUP)
        csg = cs[..., sl]
        lower = jnp.pad(csg, ((0, 0), (w, 0), (0, 0)))[:, :S]
        mean = (csg - lower) / jnp.minimum(t1, float(w))
        outs.append(mean - vf[..., sl])
    u = jnp.stack(outs, axis=2).astype(v.dtype)
    z = jnp.einsum("bsgc,gcd->bsgd", u, pool_w).reshape(B, S, POOL_WIDTH)
    return z * pool_scale


def token_mixer(h, w_in, w_out, pool_w, pool_scale, sc_conv, cf_conv, cf_ln_g, cf_ln_b):
    p = h @ w_in
    v, sc_b, sc_c, sc_h, cf_a, cf_gate = jnp.split(p, IN_SPLITS, axis=-1)
    y_pool = pool_mixer(v, pool_w, pool_scale)
    y_sc = sc_b * causal_depthwise_conv(sc_c * sc_h, sc_conv)
    u = cf_a * jax.nn.sigmoid(cf_gate)
    u = causal_depthwise_conv(u, cf_conv)
    y_cf = jax.nn.silu(layer_norm(u, cf_ln_g, cf_ln_b))
    mixed = jnp.concatenate([y_pool, y_sc, y_cf], axis=-1)
    return mixed @ w_out


def swiglu(h, w1, w3, w2):
    return (jax.nn.silu(h @ w1) * (h @ w3)) @ w2


def moe_swiglu(h, router_w, router_b, w1, w3, w2):
    B, S, D = h.shape
    hf = h.reshape(-1, D)
    logits = (hf @ router_w).astype(jnp.float32) + router_b.astype(jnp.float32)
    vals, idx = lax.top_k(logits, TOP_K)
    probs = jax.nn.softmax(vals, axis=-1)
    combine = jnp.sum(jax.nn.one_hot(idx, N_EXPERTS, dtype=jnp.float32) * probs[..., None], axis=1)
    combine = combine.astype(h.dtype)
    y = jnp.zeros_like(hf)
    for e in range(N_EXPERTS):
        y = y + combine[:, e:e + 1] * swiglu(hf, w1[e], w3[e], w2[e])
    return y.reshape(B, S, D)


def setup_inputs(seed: int = 0) -> dict:
    key = jax.random.key(seed)
    ks = jax.random.split(key, 32)
    nrm = jax.random.normal
    D, f32 = D_MODEL, jnp.float32
    inp = {}
    inp["x"] = nrm(ks[0], (BATCH, SEQ, D), f32)
    inp["c"] = nrm(ks[1], (BATCH, D), f32)
    inp["mod_w"] = nrm(ks[2], (DEPTH, D, N_MOD * D), f32) * (0.5 * D ** -0.5)
    inp["mod_b"] = nrm(ks[3], (DEPTH, N_MOD * D), f32) * 0.02
    inp["pre_mix_g"] = 1.0 + 0.05 * nrm(ks[4], (DEPTH, D), f32)
    inp["post_mix_g"] = 1.0 + 0.05 * nrm(ks[5], (DEPTH, D), f32)
    inp["pre_ffn_g"] = 1.0 + 0.05 * nrm(ks[6], (DEPTH, D), f32)
    inp["post_ffn_g"] = 1.0 + 0.05 * nrm(ks[7], (DEPTH, D), f32)
    inp["w_in"] = nrm(ks[8], (DEPTH, D, D_IN_PROJ), f32) * D ** -0.5
    inp["w_out"] = nrm(ks[9], (DEPTH, D_MIX, D), f32) * D_MIX ** -0.5
    inp["pool_w"] = nrm(ks[10], (DEPTH, len(POOL_WINDOWS), POOL_GROUP, POOL_GROUP), f32) * POOL_GROUP ** -0.5
    inp["pool_scale"] = 1.0 + 0.1 * nrm(ks[11], (DEPTH, POOL_WIDTH), f32)
    inp["sc_conv"] = nrm(ks[12], (DEPTH, SC_KERNEL, SC_WIDTH), f32) * SC_KERNEL ** -0.5
    inp["cf_conv"] = nrm(ks[13], (DEPTH, CF_KERNEL, CF_WIDTH), f32) * CF_KERNEL ** -0.5
    inp["cf_ln_g"] = 1.0 + 0.05 * nrm(ks[14], (DEPTH, CF_WIDTH), f32)
    inp["cf_ln_b"] = 0.02 * nrm(ks[15], (DEPTH, CF_WIDTH), f32)
    inp["ffn_w1"] = nrm(ks[16], (N_DENSE, D, D_FF), f32) * D ** -0.5
    inp["ffn_w3"] = nrm(ks[17], (N_DENSE, D, D_FF), f32) * D ** -0.5
    inp["ffn_w2"] = nrm(ks[18], (N_DENSE, D_FF, D), f32) * D_FF ** -0.5
    inp["router_w"] = nrm(ks[19], (N_MOE, D, N_EXPERTS), f32) * D ** -0.5
    inp["router_b"] = nrm(ks[20], (N_MOE, N_EXPERTS), f32) * 0.01
    inp["moe_w1"] = nrm(ks[21], (N_MOE, N_EXPERTS, D, D_FF_EXPERT), f32) * D ** -0.5
    inp["moe_w3"] = nrm(ks[22], (N_MOE, N_EXPERTS, D, D_FF_EXPERT), f32) * D ** -0.5
    inp["moe_w2"] = nrm(ks[23], (N_MOE, N_EXPERTS, D_FF_EXPERT, D), f32) * D_FF_EXPERT ** -0.5
    return inp


def reference(x, c, mod_w, mod_b, pre_mix_g, post_mix_g, pre_ffn_g, post_ffn_g,
              w_in, w_out, pool_w, pool_scale, sc_conv, cf_conv, cf_ln_g, cf_ln_b,
              ffn_w1, ffn_w3, ffn_w2, router_w, router_b, moe_w1, moe_w3, moe_w2):
    c_act = jax.nn.silu(c)
    for l in range(DEPTH):
        mod = (c_act @ mod_w[l] + mod_b[l])[:, None, :]
        sh1, sc1, g1, sh2, sc2, g2 = jnp.split(mod, N_MOD, axis=-1)
        h = rms_norm(x, pre_mix_g[l]) * (1.0 + sc1) + sh1
        y = token_mixer(h, w_in[l], w_out[l], pool_w[l], pool_scale[l], sc_conv[l],
                        cf_conv[l], cf_ln_g[l], cf_ln_b[l])
        x = x + g1 * rms_norm(y, post_mix_g[l])
        h = rms_norm(x, pre_ffn_g[l]) * (1.0 + sc2) + sh2
        if l % 2 == 0:
            j = l // 2
            y = swiglu(h, ffn_w1[j], ffn_w3[j], ffn_w2[j])
        else:
            j = l // 2
            y = moe_swiglu(h, router_w[j], router_b[j], moe_w1[j], moe_w3[j], moe_w2[j])
        x = x + g2 * rms_norm(y, post_ffn_g[l])
    return x
```

```python
import functools

import jax
import jax.numpy as jnp
from jax import lax
from jax.experimental import pallas as pl
from jax.experimental.pallas import tpu as pltpu

F32 = jnp.float32
BF16 = jnp.bfloat16

EPS = 1e-6
LANE = 128
POOL_WINDOWS = (2, 4, 8, 16)
POOL_GROUP = 128
POOL_WIDTH = POOL_GROUP * len(POOL_WINDOWS)
SC_KERNEL = 3
CF_KERNEL = 31
N_EXPERTS = 8
TOP_K = 2

POOL_HIST = 16
SC_HIST = 8
CF_HIST = 32

NEG_BIG = -1e30

MIB = 1024 * 1024


def _params(sem, vmem_mib):
    return pltpu.CompilerParams(dimension_semantics=sem, vmem_limit_bytes=vmem_mib * MIB)


def _modulated_rms(x, g, scale, shift):
    ms = jnp.mean(x * x, axis=-1, keepdims=True)
    return (x * lax.rsqrt(ms + EPS) * g) * (1.0 + scale) + shift


def _rms(y, g):
    ms = jnp.mean(y * y, axis=-1, keepdims=True)
    return y * lax.rsqrt(ms + EPS) * g


def _mod_kernel(c_ref, w_ref, b_ref, o_ref):
    c = c_ref[...]
    ca = (c * jax.nn.sigmoid(c)).astype(BF16)
    o_ref[0] = jnp.dot(ca, w_ref[0].astype(BF16), preferred_element_type=F32) + b_ref[0]


def _modulation(c, mod_w, mod_b):
    depth, d, n6 = mod_w.shape
    b = c.shape[0]
    tn = 1024
    c8 = jnp.pad(c, ((0, 8 - b), (0, 0)))
    out = pl.pallas_call(
        _mod_kernel,
        out_shape=jax.ShapeDtypeStruct((depth, 8, n6), F32),
        grid=(depth, n6 // tn),
        in_specs=[pl.BlockSpec((8, d), lambda l, j: (0, 0)),
                  pl.BlockSpec((1, d, tn), lambda l, j: (l, 0, j)),
                  pl.BlockSpec((1, 1, tn), lambda l, j: (l, 0, j))],
        out_specs=pl.BlockSpec((1, 8, tn), lambda l, j: (l, 0, j)),
        compiler_params=_params(("arbitrary", "arbitrary"), 40),
        name="adaln_mod",
    )(c8, mod_w, mod_b.reshape(depth, 1, n6))
    return out[:, :b].reshape(depth, b, 6, d)


def _mix_in_kernel(x_ref, mod_ref, g_ref, win_ref, poolw_ref, pscale_ref, scw_ref, cfw_ref,
                   lng_ref, lnb_ref, o_ref, p_ref, vbuf, sbuf, ubuf, *, tiles_per_seq, tm, rc,
                   sc_width, cf_width):
    i = pl.program_id(0)
    tile_in_seq = i % tiles_per_seq
    o_sc = POOL_WIDTH
    o_c = o_sc + sc_width
    o_h = o_c + sc_width
    o_a = o_h + sc_width
    o_g = o_a + cf_width

    h = _modulated_rms(x_ref[...], g_ref[...], mod_ref[0, 1:2, :], mod_ref[0, 0:1, :])
    p_ref[...] = jnp.dot(h.astype(BF16), win_ref[...], preferred_element_type=F32)

    @pl.when(tile_in_seq == 0)
    def _():
        vbuf[0:POOL_HIST, :] = jnp.zeros((POOL_HIST, POOL_WIDTH), F32)
        sbuf[0:SC_HIST, :] = jnp.zeros((SC_HIST, sc_width), F32)
        ubuf[0:CF_HIST, :] = jnp.zeros((CF_HIST, cf_width), F32)

    vbuf[POOL_HIST:POOL_HIST + tm, :] = p_ref[:, 0:POOL_WIDTH]
    sbuf[SC_HIST:SC_HIST + tm, :] = p_ref[:, o_c:o_c + sc_width] * p_ref[:, o_h:o_h + sc_width]
    ubuf[CF_HIST:CF_HIST + tm, :] = (p_ref[:, o_a:o_a + cf_width]
                                     * jax.nn.sigmoid(p_ref[:, o_g:o_g + cf_width]))

    t1 = (tile_in_seq * tm + 1 + lax.broadcasted_iota(jnp.int32, (tm, 1), 0)).astype(F32)
    for g, w in enumerate(POOL_WINDOWS):
        lanes = slice(g * POOL_GROUP, (g + 1) * POOL_GROUP)
        inv_den = 1.0 / jnp.minimum(t1, float(w))
        for r in range(tm // rc):
            r0 = r * rc
            cur = vbuf[POOL_HIST + r0:POOL_HIST + r0 + rc, lanes]
            acc = cur
            for j in range(1, w):
                acc = acc + vbuf[POOL_HIST + r0 - j:POOL_HIST + r0 - j + rc, lanes]
            u = acc * inv_den[r0:r0 + rc, :] - cur
            z = jnp.dot(u.astype(BF16), poolw_ref[g], preferred_element_type=F32)
            o_ref[r0:r0 + rc, lanes] = (z * pscale_ref[:, lanes]).astype(o_ref.dtype)

    def sc_chunk(c, carry):
        lanes = pl.ds(pl.multiple_of(c * LANE, LANE), LANE)
        for r in range(tm // rc):
            r0 = r * rc
            acc = None
            for k in range(SC_KERNEL):
                s0 = SC_HIST + r0 - (SC_KERNEL - 1) + k
                term = scw_ref[k:k + 1, lanes] * sbuf[s0:s0 + rc, lanes]
                acc = term if acc is None else acc + term
            y = p_ref[r0:r0 + rc, pl.ds(pl.multiple_of(o_sc + c * LANE, LANE), LANE)] * acc
            o_ref[r0:r0 + rc, pl.ds(pl.multiple_of(POOL_WIDTH + c * LANE, LANE), LANE)] = (
                y.astype(o_ref.dtype))
        return carry
    lax.fori_loop(0, sc_width // LANE, sc_chunk, 0)

    def cf_chunk(c, carry):
        lanes = pl.ds(pl.multiple_of(c * LANE, LANE), LANE)
        for r in range(tm // rc):
            r0 = r * rc
            acc = None
            for k in range(CF_KERNEL):
                s0 = CF_HIST + r0 - (CF_KERNEL - 1) + k
                term = cfw_ref[k:k + 1, lanes] * ubuf[s0:s0 + rc, lanes]
                acc = term if acc is None else acc + term
            p_ref[r0:r0 + rc, pl.ds(pl.multiple_of(o_a + c * LANE, LANE), LANE)] = acc
        return carry
    lax.fori_loop(0, cf_width // LANE, cf_chunk, 0)

    u = p_ref[:, o_a:o_a + cf_width]
    mu = jnp.mean(u, axis=-1, keepdims=True)
    d = u - mu
    var = jnp.mean(d * d, axis=-1, keepdims=True)
    ln = d * lax.rsqrt(var + EPS) * lng_ref[...] + lnb_ref[...]
    o_ref[:, POOL_WIDTH + sc_width:] = (ln * jax.nn.sigmoid(ln)).astype(o_ref.dtype)

    vbuf[0:POOL_HIST, :] = vbuf[tm:tm + POOL_HIST, :]
    sbuf[0:SC_HIST, :] = sbuf[tm:tm + SC_HIST, :]
    ubuf[0:CF_HIST, :] = ubuf[tm:tm + CF_HIST, :]


def _mix_in(x2, mod_l, g, w_in, pool_w, pool_scale, sc_conv, cf_conv, ln_g, ln_b, *, seq):
    n, d = x2.shape
    d_in = w_in.shape[1]
    sc_width = sc_conv.shape[1]
    cf_width = cf_conv.shape[1]
    tm = 256
    tiles_per_seq = seq // tm
    const = lambda i: (0, 0)
    return pl.pallas_call(
        functools.partial(_mix_in_kernel, tiles_per_seq=tiles_per_seq, tm=tm, rc=64,
                          sc_width=sc_width, cf_width=cf_width),
        out_shape=jax.ShapeDtypeStruct((n, d), BF16),
        grid=(n // tm,),
        in_specs=[pl.BlockSpec((tm, d), lambda i: (i, 0)),
                  pl.BlockSpec((1, 6, d), lambda i: (i // tiles_per_seq, 0, 0)),
                  pl.BlockSpec((1, d), const),
                  pl.BlockSpec((d, d_in), const, pipeline_mode=pl.Buffered(1)),
                  pl.BlockSpec(pool_w.shape, lambda i: (0, 0, 0)),
                  pl.BlockSpec((1, POOL_WIDTH), const),
                  pl.BlockSpec(sc_conv.shape, const),
                  pl.BlockSpec(cf_conv.shape, const),
                  pl.BlockSpec((1, cf_width), const),
                  pl.BlockSpec((1, cf_width), const)],
        out_specs=pl.BlockSpec((tm, d), lambda i: (i, 0)),
        scratch_shapes=[pltpu.VMEM((tm, d_in), F32),
                        pltpu.VMEM((POOL_HIST + tm, POOL_WIDTH), F32),
                        pltpu.VMEM((SC_HIST + tm, sc_width), F32),
                        pltpu.VMEM((CF_HIST + tm, cf_width), F32)],
        compiler_params=_params(("arbitrary",), 48),
        name="mix_in",
    )(x2, mod_l, g.reshape(1, d), w_in, pool_w, pool_scale.reshape(1, -1), sc_conv, cf_conv,
      ln_g.reshape(1, -1), ln_b.reshape(1, -1))


def _mix_out_kernel(m_ref, w_ref, x_ref, mod_ref, g_ref, o_ref):
    y = jnp.dot(m_ref[...], w_ref[...], preferred_element_type=F32)
    o_ref[...] = x_ref[...] + mod_ref[0, 2:3, :] * _rms(y, g_ref[...])


def _mix_out(mixed, w_out, x2, mod_l, g, *, seq):
    n, d = x2.shape
    tm = 512
    tiles_per_seq = seq // tm
    return pl.pallas_call(
        _mix_out_kernel,
        out_shape=jax.ShapeDtypeStruct((n, d), F32),
        grid=(n // tm,),
        in_specs=[pl.BlockSpec((tm, d), lambda i: (i, 0)),
                  pl.BlockSpec((d, d), lambda i: (0, 0), pipeline_mode=pl.Buffered(1)),
                  pl.BlockSpec((tm, d), lambda i: (i, 0)),
                  pl.BlockSpec((1, 6, d), lambda i: (i // tiles_per_seq, 0, 0)),
                  pl.BlockSpec((1, d), lambda i: (0, 0))],
        out_specs=pl.BlockSpec((tm, d), lambda i: (i, 0)),
        compiler_params=_params(("arbitrary",), 48),
        name="mix_out",
    )(mixed, w_out, x2, mod_l, g.reshape(1, d))


def _swiglu_step(h_scr, w1_ref, w3_ref, w2_ref, o_ref, f):
    h = h_scr[...]
    a1 = jnp.dot(h, w1_ref[...], preferred_element_type=F32)
    a3 = jnp.dot(h, w3_ref[...], preferred_element_type=F32)
    act = (a1 * jax.nn.sigmoid(a1) * a3).astype(BF16)
    contrib = jnp.dot(act, w2_ref[...], preferred_element_type=F32)

    @pl.when(f == 0)
    def _():
        o_ref[...] = contrib

    @pl.when(f > 0)
    def _():
        o_ref[...] += contrib


def _dense_ffn_kernel(x_ref, mod_ref, gpre_ref, gpost_ref, w1_ref, w3_ref, w2_ref, o_ref, h_scr, *, nf):
    f = pl.program_id(1)

    @pl.when(f == 0)
    def _():
        h = _modulated_rms(x_ref[...], gpre_ref[...], mod_ref[0, 4:5, :], mod_ref[0, 3:4, :])
        h_scr[...] = h.astype(BF16)

    _swiglu_step(h_scr, w1_ref, w3_ref, w2_ref, o_ref, f)

    @pl.when(f == nf - 1)
    def _():
        o_ref[...] = x_ref[...] + mod_ref[0, 5:6, :] * _rms(o_ref[...], gpost_ref[...])


def _dense_ffn(x2, mod_l, g_pre, g_post, w1, w3, w2, *, seq):
    n, d = x2.shape
    dff = w1.shape[1]
    tm, tf = 512, 512
    nf = dff // tf
    tiles_per_seq = seq // tm
    return pl.pallas_call(
        functools.partial(_dense_ffn_kernel, nf=nf),
        out_shape=jax.ShapeDtypeStruct((n, d), F32),
        grid=(n // tm, nf),
        in_specs=[pl.BlockSpec((tm, d), lambda i, f: (i, 0)),
                  pl.BlockSpec((1, 6, d), lambda i, f: (i // tiles_per_seq, 0, 0)),
                  pl.BlockSpec((1, d), lambda i, f: (0, 0)),
                  pl.BlockSpec((1, d), lambda i, f: (0, 0)),
                  pl.BlockSpec((d, tf), lambda i, f: (0, f)),
                  pl.BlockSpec((d, tf), lambda i, f: (0, f)),
                  pl.BlockSpec((tf, d), lambda i, f: (f, 0))],
        out_specs=pl.BlockSpec((tm, d), lambda i, f: (i, 0)),
        scratch_shapes=[pltpu.VMEM((tm, d), BF16)],
        compiler_params=_params(("arbitrary", "arbitrary"), 48),
        name="dense_swiglu",
    )(x2, mod_l, g_pre.reshape(1, d), g_post.reshape(1, d), w1, w3, w2)


def _expert_ffn_kernel(te_ref, nv_ref, hs_ref, w1_ref, w3_ref, w2_ref, o_ref, h_scr):
    i = pl.program_id(0)
    f = pl.program_id(1)

    @pl.when(i < nv_ref[0])
    def _():
        @pl.when(f == 0)
        def _():
            h_scr[...] = hs_ref[...].astype(BF16)

        _swiglu_step(h_scr, w1_ref, w3_ref, w2_ref, o_ref, f)

    @pl.when(jnp.logical_and(i >= nv_ref[0], f == 0))
    def _():
        o_ref[...] = jnp.zeros_like(o_ref)


def _expert_ffn(tile_expert, n_valid, hs, w1, w3, w2, *, tm):
    p, d = hs.shape
    dff = w1.shape[2]
    tf = 512
    nf = dff // tf
    n_tiles = p // tm

    def row_map(i, f, te, nv):
        return (i, 0)


    def f_of(i, f, nv):
        return jnp.where(i < nv[0], f, nf - 1)

    grid_spec = pltpu.PrefetchScalarGridSpec(
        num_scalar_prefetch=2,
        grid=(n_tiles, nf),
        in_specs=[pl.BlockSpec((tm, d), row_map),
                  pl.BlockSpec((None, d, tf), lambda i, f, te, nv: (te[i], 0, f_of(i, f, nv))),
                  pl.BlockSpec((None, d, tf), lambda i, f, te, nv: (te[i], 0, f_of(i, f, nv))),
                  pl.BlockSpec((None, tf, d), lambda i, f, te, nv: (te[i], f_of(i, f, nv), 0))],
        out_specs=pl.BlockSpec((tm, d), row_map),
        scratch_shapes=[pltpu.VMEM((tm, d), BF16)],
    )
    return pl.pallas_call(
        _expert_ffn_kernel,
        out_shape=jax.ShapeDtypeStruct((p, d), F32),
        grid_spec=grid_spec,
        compiler_params=_params(("arbitrary", "arbitrary"), 48),
        name="expert_swiglu",
    )(tile_expert, n_valid, hs, w1, w3, w2)


def _router_kernel(x_ref, mod_ref, g_ref, rw_ref, rb_ref, h_ref, route_ref, cnt_ref, carry, *, tm):
    i = pl.program_id(0)

    @pl.when(i == 0)
    def _():
        carry[...] = jnp.zeros_like(carry)

    h = _modulated_rms(x_ref[...], g_ref[...], mod_ref[0, 4:5, :], mod_ref[0, 3:4, :])
    h_ref[...] = h
    logits = jnp.dot(h.astype(BF16), rw_ref[...], preferred_element_type=F32) + rb_ref[...]
    lane = lax.broadcasted_iota(jnp.int32, (tm, LANE), 1)
    m1 = jnp.max(logits, axis=-1, keepdims=True)
    i1 = jnp.min(jnp.where(logits == m1, lane, LANE), axis=-1, keepdims=True)
    rest = jnp.where(lane == i1, -jnp.inf, logits)
    m2 = jnp.max(rest, axis=-1, keepdims=True)
    i2 = jnp.min(jnp.where(rest == m2, lane, LANE), axis=-1, keepdims=True)
    e = jnp.exp(m2 - m1)
    p1 = 1.0 / (1.0 + e)
    p2 = e / (1.0 + e)

    sel1 = lane == i1
    sel2 = lane == i2
    sel = jnp.logical_or(sel1, sel2).astype(F32)
    rows = lax.broadcasted_iota(jnp.int32, (tm, tm), 0)
    cols = lax.broadcasted_iota(jnp.int32, (tm, tm), 1)
    lower = (rows > cols).astype(BF16)
    rank = jnp.dot(lower, sel.astype(BF16), preferred_element_type=F32) + carry[...]
    r1 = jnp.sum(jnp.where(sel1, rank, 0.0), axis=-1, keepdims=True)
    r2 = jnp.sum(jnp.where(sel2, rank, 0.0), axis=-1, keepdims=True)
    carry[...] += jnp.sum(sel, axis=0, keepdims=True)
    cnt_ref[...] = carry[...]

    slab = jnp.where(lane == 0, i1.astype(F32), 0.0)
    slab = jnp.where(lane == 1, i2.astype(F32), slab)
    slab = jnp.where(lane == 2, r1, slab)
    slab = jnp.where(lane == 3, r2, slab)
    slab = jnp.where(lane == 4, p1, slab)
    slab = jnp.where(lane == 5, p2, slab)
    route_ref[...] = slab


def _router(x2, mod_l, g, rw_pad, rb_pad, *, seq):
    n, d = x2.shape
    tm = 512
    tiles_per_seq = seq // tm
    return pl.pallas_call(
        functools.partial(_router_kernel, tm=tm),
        out_shape=(jax.ShapeDtypeStruct((n, d), F32),
                   jax.ShapeDtypeStruct((n, LANE), F32),
                   jax.ShapeDtypeStruct((1, LANE), F32)),
        grid=(n // tm,),
        in_specs=[pl.BlockSpec((tm, d), lambda i: (i, 0)),
                  pl.BlockSpec((1, 6, d), lambda i: (i // tiles_per_seq, 0, 0)),
                  pl.BlockSpec((1, d), lambda i: (0, 0)),
                  pl.BlockSpec((d, LANE), lambda i: (0, 0)),
                  pl.BlockSpec((1, LANE), lambda i: (0, 0))],
        out_specs=(pl.BlockSpec((tm, d), lambda i: (i, 0)),
                   pl.BlockSpec((tm, LANE), lambda i: (i, 0)),
                   pl.BlockSpec((1, LANE), lambda i: (0, 0))),
        scratch_shapes=[pltpu.VMEM((1, LANE), F32)],
        compiler_params=_params(("arbitrary",), 40),
        name="router",
    )(x2, mod_l, g.reshape(1, d), rw_pad, rb_pad)


def _dispatch_kernel(pad_start_ref, pad_cnt_ref, pos_ref, h_ref, hs_ref, zero_scr, sem, zsem, *, tm):
    i = pl.program_id(0)

    def row_copy(r, k):
        return pltpu.make_async_copy(h_ref.at[pl.ds(r, 1), :],
                                     hs_ref.at[pl.ds(pos_ref[TOP_K * r + k], 1), :], sem)

    def issue(r, carry):
        for k in range(TOP_K):
            row_copy(r, k).start()
        return carry
    lax.fori_loop(0, tm, issue, 0)

    @pl.when(i == 0)
    def _():
        zero_scr[...] = jnp.zeros_like(zero_scr)
        for e in range(N_EXPERTS + 1):
            def zcopy(j, e=e):
                return pltpu.make_async_copy(zero_scr.at[pl.ds(0, 1), :],
                                             hs_ref.at[pl.ds(pad_start_ref[e] + j, 1), :], zsem)

            def zissue(j, carry):
                zcopy(j).start()
                return carry

            def zwait(j, carry):
                zcopy(j).wait()
                return carry
            lax.fori_loop(0, pad_cnt_ref[e], zissue, 0)
            lax.fori_loop(0, pad_cnt_ref[e], zwait, 0)

    def drain(r, carry):
        for k in range(TOP_K):
            row_copy(r, k).wait()
        return carry
    lax.fori_loop(0, tm, drain, 0)


def _dispatch(pad_start, pad_cnt, pos_flat, h2, *, p_rows):
    n, d = h2.shape
    tm = 512
    grid_spec = pltpu.PrefetchScalarGridSpec(
        num_scalar_prefetch=2,
        grid=(n // tm,),
        in_specs=[pl.BlockSpec((TOP_K * tm,), lambda i, ps, pc: (i,), memory_space=pltpu.SMEM),
                  pl.BlockSpec((tm, d), lambda i, ps, pc: (i, 0))],
        out_specs=pl.BlockSpec(memory_space=pl.ANY),
        scratch_shapes=[pltpu.VMEM((8, d), F32),
                        pltpu.SemaphoreType.DMA(()),
                        pltpu.SemaphoreType.DMA(())],
    )
    return pl.pallas_call(
        functools.partial(_dispatch_kernel, tm=tm),
        out_shape=jax.ShapeDtypeStruct((p_rows, d), F32),
        grid_spec=grid_spec,
        compiler_params=_params(("arbitrary",), 32),
        name="dispatch_rows",
    )(pad_start, pad_cnt, pos_flat, h2)


def _combine_kernel(pos_ref, os_ref, x_ref, route_ref, mod_ref, g_ref, o_ref, a_scr, b_scr, sem, *, tm):
    def row_copy(r, k):
        dst = a_scr if k == 0 else b_scr
        return pltpu.make_async_copy(os_ref.at[pl.ds(pos_ref[TOP_K * r + k], 1), :],
                                     dst.at[pl.ds(r, 1), :], sem)

    def issue(r, carry):
        for k in range(TOP_K):
            row_copy(r, k).start()
        return carry
    lax.fori_loop(0, tm, issue, 0)

    def drain(r, carry):
        for k in range(TOP_K):
            row_copy(r, k).wait()
        return carry
    lax.fori_loop(0, tm, drain, 0)

    y = route_ref[:, 4:5] * a_scr[...] + route_ref[:, 5:6] * b_scr[...]
    o_ref[...] = x_ref[...] + mod_ref[0, 5:6, :] * _rms(y, g_ref[...])


def _combine(pos_flat, o_sorted, x2, route, mod_l, g, *, seq):
    n, d = x2.shape
    tm = 512
    tiles_per_seq = seq // tm
    return pl.pallas_call(
        functools.partial(_combine_kernel, tm=tm),
        out_shape=jax.ShapeDtypeStruct((n, d), F32),
        grid=(n // tm,),
        in_specs=[pl.BlockSpec((TOP_K * tm,), lambda i: (i,), memory_space=pltpu.SMEM),
                  pl.BlockSpec(memory_space=pl.ANY),
                  pl.BlockSpec((tm, d), lambda i: (i, 0)),
                  pl.BlockSpec((tm, LANE), lambda i: (i, 0)),
                  pl.BlockSpec((1, 6, d), lambda i: (i // tiles_per_seq, 0, 0)),
                  pl.BlockSpec((1, d), lambda i: (0, 0))],
        out_specs=pl.BlockSpec((tm, d), lambda i: (i, 0)),
        scratch_shapes=[pltpu.VMEM((tm, d), F32),
                        pltpu.VMEM((tm, d), F32),
                        pltpu.SemaphoreType.DMA(())],
        compiler_params=_params(("arbitrary",), 40),
        name="combine_rows",
    )(pos_flat, o_sorted, x2, route, mod_l, g.reshape(1, d))


def _moe_ffn(x2, mod_l, g_pre, g_post, router_w, router_b, w1, w3, w2, *, seq):
    n, d = x2.shape
    tm = 512
    n_tiles = (TOP_K * n) // tm + N_EXPERTS
    rw_pad = jnp.pad(router_w, ((0, 0), (0, LANE - N_EXPERTS))).astype(BF16)
    rb_pad = jnp.pad(router_b.reshape(1, -1), ((0, 0), (0, LANE - N_EXPERTS)),
                     constant_values=NEG_BIG)
    h2, route, cnt = _router(x2, mod_l, g_pre, rw_pad, rb_pad, seq=seq)

    counts = cnt[0, :N_EXPERTS].astype(jnp.int32)
    tiles_e = (counts + tm - 1) // tm
    tile_end = jnp.cumsum(tiles_e)
    group_start = (tile_end - tiles_e) * tm
    n_valid = tile_end[-1:]
    tile_ids = jnp.minimum(jnp.arange(n_tiles, dtype=jnp.int32), n_valid[0] - 1)
    tile_expert = jnp.sum(tile_ids[:, None] >= tile_end[None, :], axis=1).astype(jnp.int32)
    e_idx = route[:, 0:TOP_K].astype(jnp.int32)
    rank = route[:, TOP_K:2 * TOP_K].astype(jnp.int32)
    pos_flat = (group_start[e_idx] + rank).reshape(-1)

    pad_start = jnp.concatenate([group_start + counts, n_valid * tm])
    pad_cnt = jnp.concatenate([tiles_e * tm - counts, (n_tiles - n_valid) * tm])
    hs = _dispatch(pad_start, pad_cnt, pos_flat, h2, p_rows=n_tiles * tm)
    o_sorted = _expert_ffn(tile_expert, n_valid, hs, w1, w3, w2, tm=tm)
    return _combine(pos_flat, o_sorted, x2, route, mod_l, g_post, seq=seq)


def kernel(x, c, mod_w, mod_b, pre_mix_g, post_mix_g, pre_ffn_g, post_ffn_g, w_in, w_out, pool_w,
           pool_scale, sc_conv, cf_conv, cf_ln_g, cf_ln_b, ffn_w1, ffn_w3, ffn_w2, router_w,
           router_b, moe_w1, moe_w3, moe_w2):
    b, s, d = x.shape
    depth = mod_w.shape[0]
    mod = _modulation(c, mod_w, mod_b)
    x2 = x.reshape(b * s, d)
    for l in range(depth):
        mixed = _mix_in(x2, mod[l], pre_mix_g[l], w_in[l].astype(BF16), pool_w[l].astype(BF16),
                        pool_scale[l], sc_conv[l], cf_conv[l], cf_ln_g[l], cf_ln_b[l], seq=s)
        x2 = _mix_out(mixed, w_out[l].astype(BF16), x2, mod[l], post_mix_g[l], seq=s)
        j = l // 2
        if l % 2 == 0:
            x2 = _dense_ffn(x2, mod[l], pre_ffn_g[l], post_ffn_g[l], ffn_w1[j].astype(BF16),
                            ffn_w3[j].astype(BF16), ffn_w2[j].astype(BF16), seq=s)
        else:
            x2 = _moe_ffn(x2, mod[l], pre_ffn_g[l], post_ffn_g[l], router_w[j], router_b[j],
                          moe_w1[j].astype(BF16), moe_w3[j].astype(BF16), moe_w2[j].astype(BF16),
                          seq=s)
    return x2.reshape(b, s, d)
```

```python
import functools

import jax
import jax.numpy as jnp
from jax import lax
from jax.experimental import pallas as pl
from jax.experimental.pallas import tpu as pltpu

F32 = jnp.float32
BF16 = jnp.bfloat16

EPS = 1e-6
LANE = 128
POOL_WINDOWS = (2, 4, 8, 16)
POOL_GROUP = 128
POOL_WIDTH = POOL_GROUP * len(POOL_WINDOWS)
SC_KERNEL = 3
CF_KERNEL = 31
N_EXPERTS = 8
TOP_K = 2

POOL_HIST = 16
SC_HIST = 8
CF_HIST = 32

NEG_BIG = -1e30

DENSE_FFN_ROWS = 512
EXPERT_ROWS = 1024
FFN_COLS = 512

MIB = 1024 * 1024


def _params(sem, vmem_mib):
    return pltpu.CompilerParams(dimension_semantics=sem, vmem_limit_bytes=vmem_mib * MIB)


def _modulated_rms(x, g, scale, shift):
    ms = jnp.mean(x * x, axis=-1, keepdims=True)
    return (x * lax.rsqrt(ms + EPS) * g) * (1.0 + scale) + shift


def _rms(y, g):
    ms = jnp.mean(y * y, axis=-1, keepdims=True)
    return y * lax.rsqrt(ms + EPS) * g


def _mod_kernel(c_ref, w_ref, b_ref, o_ref):
    c = c_ref[...]
    ca = (c * jax.nn.sigmoid(c)).astype(BF16)
    o_ref[0] = jnp.dot(ca, w_ref[0].astype(BF16), preferred_element_type=F32) + b_ref[0]


def _modulation(c, mod_w, mod_b):
    depth, d, n6 = mod_w.shape
    b = c.shape[0]
    tn = 1024
    c8 = jnp.pad(c, ((0, 8 - b), (0, 0)))
    out = pl.pallas_call(
        _mod_kernel,
        out_shape=jax.ShapeDtypeStruct((depth, 8, n6), F32),
        grid=(depth, n6 // tn),
        in_specs=[pl.BlockSpec((8, d), lambda l, j: (0, 0)),
                  pl.BlockSpec((1, d, tn), lambda l, j: (l, 0, j)),
                  pl.BlockSpec((1, 1, tn), lambda l, j: (l, 0, j))],
        out_specs=pl.BlockSpec((1, 8, tn), lambda l, j: (l, 0, j)),
        compiler_params=_params(("arbitrary", "arbitrary"), 40),
        name="adaln_mod",
    )(c8, mod_w, mod_b.reshape(depth, 1, n6))
    return out[:, :b].reshape(depth, b, 6, d)


def _mix_in_kernel(x_ref, mod_ref, g_ref, win_ref, poolw_ref, pscale_ref, scw_ref, cfw_ref,
                   lng_ref, lnb_ref, o_ref, p_ref, vbuf, sbuf, ubuf, *, tiles_per_seq, tm, rc,
                   sc_width, cf_width):
    i = pl.program_id(0)
    tile_in_seq = i % tiles_per_seq
    o_sc = POOL_WIDTH
    o_c = o_sc + sc_width
    o_h = o_c + sc_width
    o_a = o_h + sc_width
    o_g = o_a + cf_width

    h = _modulated_rms(x_ref[...], g_ref[...], mod_ref[0, 1:2, :], mod_ref[0, 0:1, :])
    p_ref[...] = jnp.dot(h.astype(BF16), win_ref[...], preferred_element_type=F32)

    @pl.when(tile_in_seq == 0)
    def _():
        vbuf[0:POOL_HIST, :] = jnp.zeros((POOL_HIST, POOL_WIDTH), F32)
        sbuf[0:SC_HIST, :] = jnp.zeros((SC_HIST, sc_width), F32)
        ubuf[0:CF_HIST, :] = jnp.zeros((CF_HIST, cf_width), F32)

    vbuf[POOL_HIST:POOL_HIST + tm, :] = p_ref[:, 0:POOL_WIDTH]
    sbuf[SC_HIST:SC_HIST + tm, :] = p_ref[:, o_c:o_c + sc_width] * p_ref[:, o_h:o_h + sc_width]
    ubuf[CF_HIST:CF_HIST + tm, :] = (p_ref[:, o_a:o_a + cf_width]
                                     * jax.nn.sigmoid(p_ref[:, o_g:o_g + cf_width]))

    t1 = (tile_in_seq * tm + 1 + lax.broadcasted_iota(jnp.int32, (tm, 1), 0)).astype(F32)
    for g, w in enumerate(POOL_WINDOWS):
        lanes = slice(g * POOL_GROUP, (g + 1) * POOL_GROUP)
        inv_den = 1.0 / jnp.minimum(t1, float(w))
        for r in range(tm // rc):
            r0 = r * rc
            cur = vbuf[POOL_HIST + r0:POOL_HIST + r0 + rc, lanes]
            acc = cur
            for j in range(1, w):
                acc = acc + vbuf[POOL_HIST + r0 - j:POOL_HIST + r0 - j + rc, lanes]
            u = acc * inv_den[r0:r0 + rc, :] - cur
            z = jnp.dot(u.astype(BF16), poolw_ref[g], preferred_element_type=F32)
            o_ref[r0:r0 + rc, lanes] = (z * pscale_ref[:, lanes]).astype(o_ref.dtype)

    def sc_chunk(c, carry):
        lanes = pl.ds(pl.multiple_of(c * LANE, LANE), LANE)
        for r in range(tm // rc):
            r0 = r * rc
            acc = None
            for k in range(SC_KERNEL):
                s0 = SC_HIST + r0 - (SC_KERNEL - 1) + k
                term = scw_ref[k:k + 1, lanes] * sbuf[s0:s0 + rc, lanes]
                acc = term if acc is None else acc + term
            y = p_ref[r0:r0 + rc, pl.ds(pl.multiple_of(o_sc + c * LANE, LANE), LANE)] * acc
            o_ref[r0:r0 + rc, pl.ds(pl.multiple_of(POOL_WIDTH + c * LANE, LANE), LANE)] = (
                y.astype(o_ref.dtype))
        return carry
    lax.fori_loop(0, sc_width // LANE, sc_chunk, 0)

    def cf_chunk(c, carry):
        lanes = pl.ds(pl.multiple_of(c * LANE, LANE), LANE)
        for r in range(tm // rc):
            r0 = r * rc
            acc = None
            for k in range(CF_KERNEL):
                s0 = CF_HIST + r0 - (CF_KERNEL - 1) + k
                term = cfw_ref[k:k + 1, lanes] * ubuf[s0:s0 + rc, lanes]
                acc = term if acc is None else acc + term
            p_ref[r0:r0 + rc, pl.ds(pl.multiple_of(o_a + c * LANE, LANE), LANE)] = acc
        return carry
    lax.fori_loop(0, cf_width // LANE, cf_chunk, 0)

    u = p_ref[:, o_a:o_a + cf_width]
    mu = jnp.mean(u, axis=-1, keepdims=True)
    d = u - mu
    var = jnp.mean(d * d, axis=-1, keepdims=True)
    ln = d * lax.rsqrt(var + EPS) * lng_ref[...] + lnb_ref[...]
    o_ref[:, POOL_WIDTH + sc_width:] = (ln * jax.nn.sigmoid(ln)).astype(o_ref.dtype)

    vbuf[0:POOL_HIST, :] = vbuf[tm:tm + POOL_HIST, :]
    sbuf[0:SC_HIST, :] = sbuf[tm:tm + SC_HIST, :]
    ubuf[0:CF_HIST, :] = ubuf[tm:tm + CF_HIST, :]


def _mix_in(x2, mod_l, g, w_in, pool_w, pool_scale, sc_conv, cf_conv, ln_g, ln_b, *, seq):
    n, d = x2.shape
    d_in = w_in.shape[1]
    sc_width = sc_conv.shape[1]
    cf_width = cf_conv.shape[1]
    tm = 256
    tiles_per_seq = seq // tm
    const = lambda i: (0, 0)
    return pl.pallas_call(
        functools.partial(_mix_in_kernel, tiles_per_seq=tiles_per_seq, tm=tm, rc=64,
                          sc_width=sc_width, cf_width=cf_width),
        out_shape=jax.ShapeDtypeStruct((n, d), BF16),
        grid=(n // tm,),
        in_specs=[pl.BlockSpec((tm, d), lambda i: (i, 0)),
                  pl.BlockSpec((1, 6, d), lambda i: (i // tiles_per_seq, 0, 0)),
                  pl.BlockSpec((1, d), const),
                  pl.BlockSpec((d, d_in), const, pipeline_mode=pl.Buffered(1)),
                  pl.BlockSpec(pool_w.shape, lambda i: (0, 0, 0)),
                  pl.BlockSpec((1, POOL_WIDTH), const),
                  pl.BlockSpec(sc_conv.shape, const),
                  pl.BlockSpec(cf_conv.shape, const),
                  pl.BlockSpec((1, cf_width), const),
                  pl.BlockSpec((1, cf_width), const)],
        out_specs=pl.BlockSpec((tm, d), lambda i: (i, 0)),
        scratch_shapes=[pltpu.VMEM((tm, d_in), F32),
                        pltpu.VMEM((POOL_HIST + tm, POOL_WIDTH), F32),
                        pltpu.VMEM((SC_HIST + tm, sc_width), F32),
                        pltpu.VMEM((CF_HIST + tm, cf_width), F32)],
        compiler_params=_params(("arbitrary",), 48),
        name="mix_in",
    )(x2, mod_l, g.reshape(1, d), w_in, pool_w, pool_scale.reshape(1, -1), sc_conv, cf_conv,
      ln_g.reshape(1, -1), ln_b.reshape(1, -1))


def _mix_out_kernel(m_ref, w_ref, x_ref, mod_ref, g_ref, o_ref):
    y = jnp.dot(m_ref[...], w_ref[...], preferred_element_type=F32)
    o_ref[...] = x_ref[...] + mod_ref[0, 2:3, :] * _rms(y, g_ref[...])


def _mix_out(mixed, w_out, x2, mod_l, g, *, seq):
    n, d = x2.shape
    tm = 512
    tiles_per_seq = seq // tm
    return pl.pallas_call(
        _mix_out_kernel,
        out_shape=jax.ShapeDtypeStruct((n, d), F32),
        grid=(n // tm,),
        in_specs=[pl.BlockSpec((tm, d), lambda i: (i, 0)),
                  pl.BlockSpec((d, d), lambda i: (0, 0), pipeline_mode=pl.Buffered(1)),
                  pl.BlockSpec((tm, d), lambda i: (i, 0)),
                  pl.BlockSpec((1, 6, d), lambda i: (i // tiles_per_seq, 0, 0)),
                  pl.BlockSpec((1, d), lambda i: (0, 0))],
        out_specs=pl.BlockSpec((tm, d), lambda i: (i, 0)),
        compiler_params=_params(("arbitrary",), 48),
        name="mix_out",
    )(mixed, w_out, x2, mod_l, g.reshape(1, d))


def _swiglu_accumulate(h, w1_ref, w3_ref, w2_ref, o_ref):
    a1 = jnp.dot(h, w1_ref[...], preferred_element_type=F32)
    a3 = jnp.dot(h, w3_ref[...], preferred_element_type=F32)
    act = (a1 * jax.nn.sigmoid(a1) * a3).astype(BF16)
    o_ref[...] += jnp.dot(act, w2_ref[...], preferred_element_type=F32)


def _dense_ffn_kernel(x_ref, mod_ref, gpre_ref, gpost_ref, w1_ref, w3_ref, w2_ref, o_ref, h_scr, *, nf):
    f = pl.program_id(1)

    @pl.when(f == 0)
    def _():
        h = _modulated_rms(x_ref[...], gpre_ref[...], mod_ref[0, 4:5, :], mod_ref[0, 3:4, :])
        h_scr[...] = h.astype(BF16)
        o_ref[...] = jnp.zeros_like(o_ref)

    _swiglu_accumulate(h_scr[...], w1_ref, w3_ref, w2_ref, o_ref)

    @pl.when(f == nf - 1)
    def _():
        o_ref[...] = x_ref[...] + mod_ref[0, 5:6, :] * _rms(o_ref[...], gpost_ref[...])


def _dense_ffn(x2, mod_l, g_pre, g_post, w1, w3, w2, *, seq):
    n, d = x2.shape
    dff = w1.shape[1]
    tm, tf = DENSE_FFN_ROWS, FFN_COLS
    nf = dff // tf
    tiles_per_seq = seq // tm
    return pl.pallas_call(
        functools.partial(_dense_ffn_kernel, nf=nf),
        out_shape=jax.ShapeDtypeStruct((n, d), F32),
        grid=(n // tm, nf),
        in_specs=[pl.BlockSpec((tm, d), lambda i, f: (i, 0)),
                  pl.BlockSpec((1, 6, d), lambda i, f: (i // tiles_per_seq, 0, 0)),
                  pl.BlockSpec((1, d), lambda i, f: (0, 0)),
                  pl.BlockSpec((1, d), lambda i, f: (0, 0)),
                  pl.BlockSpec((d, tf), lambda i, f: (0, f)),
                  pl.BlockSpec((d, tf), lambda i, f: (0, f)),
                  pl.BlockSpec((tf, d), lambda i, f: (f, 0))],
        out_specs=pl.BlockSpec((tm, d), lambda i, f: (i, 0)),
        scratch_shapes=[pltpu.VMEM((tm, d), BF16)],
        compiler_params=_params(("arbitrary", "arbitrary"), 48),
        name="dense_swiglu",
    )(x2, mod_l, g_pre.reshape(1, d), g_post.reshape(1, d), w1, w3, w2)


def _expert_ffn_kernel(te_ref, nv_ref, hs_ref, w1_ref, w3_ref, w2_ref, o_ref):
    i = pl.program_id(0)
    f = pl.program_id(1)

    @pl.when(f == 0)
    def _():
        o_ref[...] = jnp.zeros_like(o_ref)

    @pl.when(i < nv_ref[0])
    def _():
        _swiglu_accumulate(hs_ref[...].astype(BF16), w1_ref, w3_ref, w2_ref, o_ref)


def _expert_ffn(tile_expert, n_valid, hs, w1, w3, w2, *, tm):
    p, d = hs.shape
    dff = w1.shape[2]
    tf = FFN_COLS
    nf = dff // tf
    n_tiles = p // tm

    def row_map(i, f, te, nv):
        return (i, 0)

    def f_of(i, f, nv):
        return jnp.where(i < nv[0], f, nf - 1)

    grid_spec = pltpu.PrefetchScalarGridSpec(
        num_scalar_prefetch=2,
        grid=(n_tiles, nf),
        in_specs=[pl.BlockSpec((tm, d), row_map),
                  pl.BlockSpec((None, d, tf), lambda i, f, te, nv: (te[i], 0, f_of(i, f, nv))),
                  pl.BlockSpec((None, d, tf), lambda i, f, te, nv: (te[i], 0, f_of(i, f, nv))),
                  pl.BlockSpec((None, tf, d), lambda i, f, te, nv: (te[i], f_of(i, f, nv), 0))],
        out_specs=pl.BlockSpec((tm, d), row_map),
    )
    return pl.pallas_call(
        _expert_ffn_kernel,
        out_shape=jax.ShapeDtypeStruct((p, d), F32),
        grid_spec=grid_spec,
        compiler_params=_params(("arbitrary", "arbitrary"), 56),
        name="expert_swiglu",
    )(tile_expert, n_valid, hs, w1, w3, w2)


def _router_kernel(x_ref, mod_ref, g_ref, rw_ref, rb_ref, h_ref, route_ref, cnt_ref, carry, *, tm):
    i = pl.program_id(0)

    @pl.when(i == 0)
    def _():
        carry[...] = jnp.zeros_like(carry)

    h = _modulated_rms(x_ref[...], g_ref[...], mod_ref[0, 4:5, :], mod_ref[0, 3:4, :])
    h_ref[...] = h
    logits = jnp.dot(h.astype(BF16), rw_ref[...], preferred_element_type=F32) + rb_ref[...]
    lane = lax.broadcasted_iota(jnp.int32, (tm, LANE), 1)
    m1 = jnp.max(logits, axis=-1, keepdims=True)
    i1 = jnp.min(jnp.where(logits == m1, lane, LANE), axis=-1, keepdims=True)
    rest = jnp.where(lane == i1, -jnp.inf, logits)
    m2 = jnp.max(rest, axis=-1, keepdims=True)
    i2 = jnp.min(jnp.where(rest == m2, lane, LANE), axis=-1, keepdims=True)
    e = jnp.exp(m2 - m1)
    p1 = 1.0 / (1.0 + e)
    p2 = e / (1.0 + e)

    sel1 = lane == i1
    sel2 = lane == i2
    sel = jnp.logical_or(sel1, sel2).astype(F32)
    rows = lax.broadcasted_iota(jnp.int32, (tm, tm), 0)
    cols = lax.broadcasted_iota(jnp.int32, (tm, tm), 1)
    lower = (rows > cols).astype(BF16)
    rank = jnp.dot(lower, sel.astype(BF16), preferred_element_type=F32) + carry[...]
    r1 = jnp.sum(jnp.where(sel1, rank, 0.0), axis=-1, keepdims=True)
    r2 = jnp.sum(jnp.where(sel2, rank, 0.0), axis=-1, keepdims=True)
    carry[...] += jnp.sum(sel, axis=0, keepdims=True)
    cnt_ref[...] = carry[...]

    slab = jnp.where(lane == 0, i1.astype(F32), 0.0)
    slab = jnp.where(lane == 1, i2.astype(F32), slab)
    slab = jnp.where(lane == 2, r1, slab)
    slab = jnp.where(lane == 3, r2, slab)
    slab = jnp.where(lane == 4, p1, slab)
    slab = jnp.where(lane == 5, p2, slab)
    route_ref[...] = slab


def _router(x2, mod_l, g, rw_pad, rb_pad, *, seq):
    n, d = x2.shape
    tm = 512
    tiles_per_seq = seq // tm
    return pl.pallas_call(
        functools.partial(_router_kernel, tm=tm),
        out_shape=(jax.ShapeDtypeStruct((n, d), F32),
                   jax.ShapeDtypeStruct((n, LANE), F32),
                   jax.ShapeDtypeStruct((1, LANE), F32)),
        grid=(n // tm,),
        in_specs=[pl.BlockSpec((tm, d), lambda i: (i, 0)),
                  pl.BlockSpec((1, 6, d), lambda i: (i // tiles_per_seq, 0, 0)),
                  pl.BlockSpec((1, d), lambda i: (0, 0)),
                  pl.BlockSpec((d, LANE), lambda i: (0, 0)),
                  pl.BlockSpec((1, LANE), lambda i: (0, 0))],
        out_specs=(pl.BlockSpec((tm, d), lambda i: (i, 0)),
                   pl.BlockSpec((tm, LANE), lambda i: (i, 0)),
                   pl.BlockSpec((1, LANE), lambda i: (0, 0))),
        scratch_shapes=[pltpu.VMEM((1, LANE), F32)],
        compiler_params=_params(("arbitrary",), 40),
        name="router",
    )(x2, mod_l, g.reshape(1, d), rw_pad, rb_pad)


def _dispatch_kernel(pad_start_ref, pad_cnt_ref, pos_ref, h_ref, hs_ref, zero_scr, sem, zsem, *, tm):
    i = pl.program_id(0)

    def row_copy(r, k):
        return pltpu.make_async_copy(h_ref.at[pl.ds(r, 1), :],
                                     hs_ref.at[pl.ds(pos_ref[TOP_K * r + k], 1), :], sem)

    def issue(r, carry):
        for k in range(TOP_K):
            row_copy(r, k).start()
        return carry
    lax.fori_loop(0, tm, issue, 0)

    @pl.when(i == 0)
    def _():
        zero_scr[...] = jnp.zeros_like(zero_scr)
        for e in range(N_EXPERTS + 1):
            def zcopy(j, e=e):
                return pltpu.make_async_copy(zero_scr.at[pl.ds(0, 1), :],
                                             hs_ref.at[pl.ds(pad_start_ref[e] + j, 1), :], zsem)

            def zissue(j, carry):
                zcopy(j).start()
                return carry

            def zwait(j, carry):
                zcopy(j).wait()
                return carry
            lax.fori_loop(0, pad_cnt_ref[e], zissue, 0)
            lax.fori_loop(0, pad_cnt_ref[e], zwait, 0)

    for k in range(TOP_K):
        pltpu.make_async_copy(h_ref, hs_ref.at[pl.ds(0, tm), :], sem).wait()


def _dispatch(pad_start, pad_cnt, pos_flat, h2, *, p_rows):
    n, d = h2.shape
    tm = 512
    grid_spec = pltpu.PrefetchScalarGridSpec(
        num_scalar_prefetch=2,
        grid=(n // tm,),
        in_specs=[pl.BlockSpec((TOP_K * tm,), lambda i, ps, pc: (i,), memory_space=pltpu.SMEM),
                  pl.BlockSpec((tm, d), lambda i, ps, pc: (i, 0))],
        out_specs=pl.BlockSpec(memory_space=pl.ANY),
        scratch_shapes=[pltpu.VMEM((8, d), F32),
                        pltpu.SemaphoreType.DMA(()),
                        pltpu.SemaphoreType.DMA(())],
    )
    return pl.pallas_call(
        functools.partial(_dispatch_kernel, tm=tm),
        out_shape=jax.ShapeDtypeStruct((p_rows, d), F32),
        grid_spec=grid_spec,
        compiler_params=_params(("arbitrary",), 32),
        name="dispatch_rows",
    )(pad_start, pad_cnt, pos_flat, h2)


def _combine_kernel(pos_ref, os_ref, x_ref, route_ref, mod_ref, g_ref, o_ref, a_scr, b_scr, sem, *, tm):
    def row_copy(r, k):
        dst = a_scr if k == 0 else b_scr
        return pltpu.make_async_copy(os_ref.at[pl.ds(pos_ref[TOP_K * r + k], 1), :],
                                     dst.at[pl.ds(r, 1), :], sem)

    def issue(r, carry):
        for k in range(TOP_K):
            row_copy(r, k).start()
        return carry
    lax.fori_loop(0, tm, issue, 0)

    for dst in (a_scr, b_scr):
        pltpu.make_async_copy(os_ref.at[pl.ds(0, tm), :], dst, sem).wait()

    y = route_ref[:, 4:5] * a_scr[...] + route_ref[:, 5:6] * b_scr[...]
    o_ref[...] = x_ref[...] + mod_ref[0, 5:6, :] * _rms(y, g_ref[...])


def _combine(pos_flat, o_sorted, x2, route, mod_l, g, *, seq):
    n, d = x2.shape
    tm = 512
    tiles_per_seq = seq // tm
    return pl.pallas_call(
        functools.partial(_combine_kernel, tm=tm),
        out_shape=jax.ShapeDtypeStruct((n, d), F32),
        grid=(n // tm,),
        in_specs=[pl.BlockSpec((TOP_K * tm,), lambda i: (i,), memory_space=pltpu.SMEM),
                  pl.BlockSpec(memory_space=pl.ANY),
                  pl.BlockSpec((tm, d), lambda i: (i, 0)),
                  pl.BlockSpec((tm, LANE), lambda i: (i, 0)),
                  pl.BlockSpec((1, 6, d), lambda i: (i // tiles_per_seq, 0, 0)),
                  pl.BlockSpec((1, d), lambda i: (0, 0))],
        out_specs=pl.BlockSpec((tm, d), lambda i: (i, 0)),
        scratch_shapes=[pltpu.VMEM((tm, d), F32),
                        pltpu.VMEM((tm, d), F32),
                        pltpu.SemaphoreType.DMA(())],
        compiler_params=_params(("arbitrary",), 40),
        name="combine_rows",
    )(pos_flat, o_sorted, x2, route, mod_l, g.reshape(1, d))


def _moe_ffn(x2, mod_l, g_pre, g_post, router_w, router_b, w1, w3, w2, *, seq):
    n, d = x2.shape
    tm = EXPERT_ROWS
    n_tiles = (TOP_K * n) // tm + N_EXPERTS
    rw_pad = jnp.pad(router_w, ((0, 0), (0, LANE - N_EXPERTS))).astype(BF16)
    rb_pad = jnp.pad(router_b.reshape(1, -1), ((0, 0), (0, LANE - N_EXPERTS)),
                     constant_values=NEG_BIG)
    h2, route, cnt = _router(x2, mod_l, g_pre, rw_pad, rb_pad, seq=seq)

    counts = cnt[0, :N_EXPERTS].astype(jnp.int32)
    tiles_e = (counts + tm - 1) // tm
    tile_end = jnp.cumsum(tiles_e)
    group_start = (tile_end - tiles_e) * tm
    n_valid = tile_end[-1:]
    tile_ids = jnp.minimum(jnp.arange(n_tiles, dtype=jnp.int32), n_valid[0] - 1)
    tile_expert = jnp.sum(tile_ids[:, None] >= tile_end[None, :], axis=1).astype(jnp.int32)
    e_idx = route[:, 0:TOP_K].astype(jnp.int32)
    rank = route[:, TOP_K:2 * TOP_K].astype(jnp.int32)
    pos_flat = (group_start[e_idx] + rank).reshape(-1)

    pad_start = jnp.concatenate([group_start + counts, n_valid * tm])
    pad_cnt = jnp.concatenate([tiles_e * tm - counts, (n_tiles - n_valid) * tm])
    hs = _dispatch(pad_start, pad_cnt, pos_flat, h2, p_rows=n_tiles * tm)
    o_sorted = _expert_ffn(tile_expert, n_valid, hs, w1, w3, w2, tm=tm)
    return _combine(pos_flat, o_sorted, x2, route, mod_l, g_post, seq=seq)


def kernel(x, c, mod_w, mod_b, pre_mix_g, post_mix_g, pre_ffn_g, post_ffn_g, w_in, w_out, pool_w,
           pool_scale, sc_conv, cf_conv, cf_ln_g, cf_ln_b, ffn_w1, ffn_w3, ffn_w2, router_w,
           router_b, moe_w1, moe_w3, moe_w2):
    b, s, d = x.shape
    depth = mod_w.shape[0]
    mod = _modulation(c, mod_w, mod_b)
    x2 = x.reshape(b * s, d)
    for l in range(depth):
        mixed = _mix_in(x2, mod[l], pre_mix_g[l], w_in[l].astype(BF16), pool_w[l].astype(BF16),
                        pool_scale[l], sc_conv[l], cf_conv[l], cf_ln_g[l], cf_ln_b[l], seq=s)
        x2 = _mix_out(mixed, w_out[l].astype(BF16), x2, mod[l], post_mix_g[l], seq=s)
        j = l // 2
        if l % 2 == 0:
            x2 = _dense_ffn(x2, mod[l], pre_ffn_g[l], post_ffn_g[l], ffn_w1[j].astype(BF16),
                            ffn_w3[j].astype(BF16), ffn_w2[j].astype(BF16), seq=s)
        else:
            x2 = _moe_ffn(x2, mod[l], pre_ffn_g[l], post_ffn_g[l], router_w[j], router_b[j],
                          moe_w1[j].astype(BF16), moe_w3[j].astype(BF16), moe_w2[j].astype(BF16),
                          seq=s)
    return x2.reshape(b, s, d)
```

```python
import functools

import jax
import jax.numpy as jnp
from jax import lax
from jax.experimental import pallas as pl
from jax.experimental.pallas import tpu as pltpu

F32 = jnp.float32
BF16 = jnp.bfloat16

EPS = 1e-6
LANE = 128
POOL_WINDOWS = (2, 4, 8, 16)
POOL_GROUP = 128
POOL_WIDTH = POOL_GROUP * len(POOL_WINDOWS)
SC_KERNEL = 3
CF_KERNEL = 31
N_EXPERTS = 8
TOP_K = 2

POOL_HIST = 16
SC_HIST = 8
CF_HIST = 32

NEG_BIG = -1e30

DENSE_FFN_ROWS = 512
EXPERT_ROWS = 1024
FFN_COLS = 512
MIX_ROWS = 256
MIX_ROW_CHUNK = 64
PROJ_PIECE = 256

MIB = 1024 * 1024


def _params(sem, vmem_mib):
    return pltpu.CompilerParams(dimension_semantics=sem, vmem_limit_bytes=vmem_mib * MIB)


def _modulated_rms(x, g, scale, shift):
    ms = jnp.mean(x * x, axis=-1, keepdims=True)
    return (x * lax.rsqrt(ms + EPS) * g) * (1.0 + scale) + shift


def _rms(y, g):
    ms = jnp.mean(y * y, axis=-1, keepdims=True)
    return y * lax.rsqrt(ms + EPS) * g


def _mod_kernel(c_ref, w_ref, b_ref, o_ref):
    c = c_ref[...]
    ca = (c * jax.nn.sigmoid(c)).astype(BF16)
    o_ref[0] = jnp.dot(ca, w_ref[0].astype(BF16), preferred_element_type=F32) + b_ref[0]


def _modulation(c, mod_w, mod_b):
    depth, d, n6 = mod_w.shape
    b = c.shape[0]
    tn = 1024
    c8 = jnp.pad(c, ((0, 8 - b), (0, 0)))
    out = pl.pallas_call(
        _mod_kernel,
        out_shape=jax.ShapeDtypeStruct((depth, 8, n6), F32),
        grid=(depth, n6 // tn),
        in_specs=[pl.BlockSpec((8, d), lambda l, j: (0, 0)),
                  pl.BlockSpec((1, d, tn), lambda l, j: (l, 0, j)),
                  pl.BlockSpec((1, 1, tn), lambda l, j: (l, 0, j))],
        out_specs=pl.BlockSpec((1, 8, tn), lambda l, j: (l, 0, j)),
        compiler_params=_params(("arbitrary", "arbitrary"), 40),
        name="adaln_mod",
    )(c8, mod_w, mod_b.reshape(depth, 1, n6))
    return out[:, :b].reshape(depth, b, 6, d)


def _mix_in_kernel(x_ref, mod_ref, g_ref, win_ref, poolw_ref, pscale_ref, scw_ref, cfw_ref,
                   lng_ref, lnb_ref, o_ref, p_ref, hb_ref, vbuf, sbuf, ubuf, bbuf, cbuf, *, tiles_per_seq,
                   n_tiles, tm, rc, sc_width, cf_width):
    i = pl.program_id(0)
    o_sc = POOL_WIDTH
    o_c = o_sc + sc_width
    o_h = o_c + sc_width
    o_a = o_h + sc_width
    o_g = o_a + cf_width

    @pl.when(i == 0)
    def _():
        vbuf[...] = jnp.zeros_like(vbuf)
        sbuf[...] = jnp.zeros_like(sbuf)
        ubuf[...] = jnp.zeros_like(ubuf)
        bbuf[...] = jnp.zeros_like(bbuf)

    staged_in_seq = (jnp.maximum(i, 1) - 1) % tiles_per_seq
    t1 = (staged_in_seq * tm + 1 + lax.broadcasted_iota(jnp.int32, (tm, 1), 0)).astype(F32)
    work = []

    def pool_item(g, w, r0, inv_den):
        lanes = slice(g * POOL_GROUP, (g + 1) * POOL_GROUP)
        cur = vbuf[POOL_HIST + r0:POOL_HIST + r0 + rc, lanes]
        acc = cur
        for j in range(1, w):
            acc = acc + vbuf[POOL_HIST + r0 - j:POOL_HIST + r0 - j + rc, lanes]
        u = acc * inv_den[r0:r0 + rc, :] - cur
        z = jnp.dot(u.astype(BF16), poolw_ref[g], preferred_element_type=F32)
        o_ref[r0:r0 + rc, lanes] = (z * pscale_ref[:, lanes]).astype(o_ref.dtype)

    for g, w in enumerate(POOL_WINDOWS):
        inv_den = 1.0 / jnp.minimum(t1, float(w))
        for r0 in range(0, tm, rc):
            work.append((4 * w, functools.partial(pool_item, g, w, r0, inv_den)))

    def sc_item(c0, r0):
        lanes = slice(c0, c0 + LANE)
        acc = None
        for k in range(SC_KERNEL):
            s0 = SC_HIST + r0 - (SC_KERNEL - 1) + k
            term = scw_ref[k:k + 1, lanes] * sbuf[s0:s0 + rc, lanes]
            acc = term if acc is None else acc + term
        y = bbuf[r0:r0 + rc, lanes] * acc
        o_ref[r0:r0 + rc, POOL_WIDTH + c0:POOL_WIDTH + c0 + LANE] = y.astype(o_ref.dtype)

    for c0 in range(0, sc_width, LANE):
        for r0 in range(0, tm, rc):
            work.append((4 * SC_KERNEL, functools.partial(sc_item, c0, r0)))

    def cf_item(c0, r0):
        lanes = slice(c0, c0 + LANE)
        base = CF_HIST + r0
        acc = None
        for r in range(8):
            lo = base - (8 if r else 0)
            rows = rc + (8 if r else 0)
            part = None
            for q in range((CF_KERNEL - 1 - r) // 8 + 1):
                k = CF_KERNEL - 1 - 8 * q - r
                term = cfw_ref[k:k + 1, lanes] * ubuf[lo - 8 * q:lo - 8 * q + rows, lanes]
                part = term if part is None else part + term
            if r:
                part = part[8 - r:8 - r + rc, :]
            acc = part if acc is None else acc + part
        cbuf[r0:r0 + rc, lanes] = acc

    for c0 in range(0, cf_width, LANE):
        for r0 in range(0, tm, rc):
            work.append((3 * CF_KERNEL, functools.partial(cf_item, c0, r0)))

    h = _modulated_rms(x_ref[...], g_ref[...], mod_ref[0, 1:2, :], mod_ref[0, 0:1, :])
    hb_ref[...] = h.astype(BF16)
    d_in = p_ref.shape[1]
    pieces = range(0, d_in, PROJ_PIECE)
    per_piece = sum(cost for cost, _ in work) / len(pieces)
    done, spent = 0, 0.0
    for j, c0 in enumerate(pieces):
        p_ref[:, c0:c0 + PROJ_PIECE] = jnp.dot(hb_ref[...], win_ref[:, c0:c0 + PROJ_PIECE],
                                               preferred_element_type=F32)
        while done < len(work) and spent < (j + 1) * per_piece:
            cost, emit = work[done]
            emit()
            spent += cost
            done += 1

    u = cbuf[...]
    mu = jnp.mean(u, axis=-1, keepdims=True)
    d = u - mu
    var = jnp.mean(d * d, axis=-1, keepdims=True)
    ln = d * lax.rsqrt(var + EPS) * lng_ref[...] + lnb_ref[...]
    o_ref[:, POOL_WIDTH + sc_width:] = (ln * jax.nn.sigmoid(ln)).astype(o_ref.dtype)

    keep = (jnp.minimum(i, n_tiles - 1) % tiles_per_seq) != 0
    vbuf[0:POOL_HIST, :] = jnp.where(keep, vbuf[tm:tm + POOL_HIST, :], 0.0)
    sbuf[0:SC_HIST, :] = jnp.where(keep, sbuf[tm:tm + SC_HIST, :], 0.0)
    ubuf[0:CF_HIST, :] = jnp.where(keep, ubuf[tm:tm + CF_HIST, :], 0.0)
    vbuf[POOL_HIST:POOL_HIST + tm, :] = p_ref[:, 0:POOL_WIDTH]
    bbuf[...] = p_ref[:, o_sc:o_sc + sc_width]
    sbuf[SC_HIST:SC_HIST + tm, :] = p_ref[:, o_c:o_c + sc_width] * p_ref[:, o_h:o_h + sc_width]
    ubuf[CF_HIST:CF_HIST + tm, :] = (p_ref[:, o_a:o_a + cf_width]
                                     * jax.nn.sigmoid(p_ref[:, o_g:o_g + cf_width]))


def _mix_in(x2, mod_l, g, w_in, pool_w, pool_scale, sc_conv, cf_conv, ln_g, ln_b, *, seq):
    n, d = x2.shape
    d_in = w_in.shape[1]
    sc_width = sc_conv.shape[1]
    cf_width = cf_conv.shape[1]
    tm = MIX_ROWS
    n_tiles = n // tm
    tiles_per_seq = seq // tm
    const = lambda i: (0, 0)
    cur = lambda i: jnp.minimum(i, n_tiles - 1)
    return pl.pallas_call(
        functools.partial(_mix_in_kernel, tiles_per_seq=tiles_per_seq, n_tiles=n_tiles, tm=tm,
                          rc=MIX_ROW_CHUNK, sc_width=sc_width, cf_width=cf_width),
        out_shape=jax.ShapeDtypeStruct((n, d), BF16),
        grid=(n_tiles + 1,),
        in_specs=[pl.BlockSpec((tm, d), lambda i: (cur(i), 0)),
                  pl.BlockSpec((1, 6, d), lambda i: (cur(i) // tiles_per_seq, 0, 0)),
                  pl.BlockSpec((1, d), const),
                  pl.BlockSpec((d, d_in), const, pipeline_mode=pl.Buffered(1)),
                  pl.BlockSpec(pool_w.shape, lambda i: (0, 0, 0)),
                  pl.BlockSpec((1, POOL_WIDTH), const),
                  pl.BlockSpec(sc_conv.shape, const),
                  pl.BlockSpec(cf_conv.shape, const),
                  pl.BlockSpec((1, cf_width), const),
                  pl.BlockSpec((1, cf_width), const)],
        out_specs=pl.BlockSpec((tm, d), lambda i: (jnp.maximum(i, 1) - 1, 0)),
        scratch_shapes=[pltpu.VMEM((tm, d_in), F32),
                        pltpu.VMEM((tm, d), BF16),
                        pltpu.VMEM((POOL_HIST + tm, POOL_WIDTH), F32),
                        pltpu.VMEM((SC_HIST + tm, sc_width), F32),
                        pltpu.VMEM((CF_HIST + tm, cf_width), F32),
                        pltpu.VMEM((tm, sc_width), F32),
                        pltpu.VMEM((tm, cf_width), F32)],
        compiler_params=_params(("arbitrary",), 48),
        name="mix_in",
    )(x2, mod_l, g.reshape(1, d), w_in, pool_w, pool_scale.reshape(1, -1), sc_conv, cf_conv,
      ln_g.reshape(1, -1), ln_b.reshape(1, -1))


def _mix_out_kernel(m_ref, w_ref, x_ref, mod_ref, g_ref, o_ref):
    y = jnp.dot(m_ref[...], w_ref[...], preferred_element_type=F32)
    o_ref[...] = x_ref[...] + mod_ref[0, 2:3, :] * _rms(y, g_ref[...])


def _mix_out(mixed, w_out, x2, mod_l, g, *, seq):
    n, d = x2.shape
    tm = 512
    tiles_per_seq = seq // tm
    return pl.pallas_call(
        _mix_out_kernel,
        out_shape=jax.ShapeDtypeStruct((n, d), F32),
        grid=(n // tm,),
        in_specs=[pl.BlockSpec((tm, d), lambda i: (i, 0)),
                  pl.BlockSpec((d, d), lambda i: (0, 0), pipeline_mode=pl.Buffered(1)),
                  pl.BlockSpec((tm, d), lambda i: (i, 0)),
                  pl.BlockSpec((1, 6, d), lambda i: (i // tiles_per_seq, 0, 0)),
                  pl.BlockSpec((1, d), lambda i: (0, 0))],
        out_specs=pl.BlockSpec((tm, d), lambda i: (i, 0)),
        compiler_params=_params(("arbitrary",), 48),
        name="mix_out",
    )(mixed, w_out, x2, mod_l, g.reshape(1, d))


def _swiglu_accumulate(h, w1_ref, w3_ref, w2_ref, o_ref):
    a1 = jnp.dot(h, w1_ref[...], preferred_element_type=F32)
    a3 = jnp.dot(h, w3_ref[...], preferred_element_type=F32)
    act = (a1 * jax.nn.sigmoid(a1) * a3).astype(BF16)
    o_ref[...] += jnp.dot(act, w2_ref[...], preferred_element_type=F32)


def _dense_ffn_kernel(x_ref, mod_ref, gpre_ref, gpost_ref, w1_ref, w3_ref, w2_ref, o_ref, h_scr, *, nf):
    f = pl.program_id(1)

    @pl.when(f == 0)
    def _():
        h = _modulated_rms(x_ref[...], gpre_ref[...], mod_ref[0, 4:5, :], mod_ref[0, 3:4, :])
        h_scr[...] = h.astype(BF16)
        o_ref[...] = jnp.zeros_like(o_ref)

    _swiglu_accumulate(h_scr[...], w1_ref, w3_ref, w2_ref, o_ref)

    @pl.when(f == nf - 1)
    def _():
        o_ref[...] = x_ref[...] + mod_ref[0, 5:6, :] * _rms(o_ref[...], gpost_ref[...])


def _dense_ffn(x2, mod_l, g_pre, g_post, w1, w3, w2, *, seq):
    n, d = x2.shape
    dff = w1.shape[1]
    tm, tf = DENSE_FFN_ROWS, FFN_COLS
    nf = dff // tf
    tiles_per_seq = seq // tm
    return pl.pallas_call(
        functools.partial(_dense_ffn_kernel, nf=nf),
        out_shape=jax.ShapeDtypeStruct((n, d), F32),
        grid=(n // tm, nf),
        in_specs=[pl.BlockSpec((tm, d), lambda i, f: (i, 0)),
                  pl.BlockSpec((1, 6, d), lambda i, f: (i // tiles_per_seq, 0, 0)),
                  pl.BlockSpec((1, d), lambda i, f: (0, 0)),
                  pl.BlockSpec((1, d), lambda i, f: (0, 0)),
                  pl.BlockSpec((d, tf), lambda i, f: (0, f)),
                  pl.BlockSpec((d, tf), lambda i, f: (0, f)),
                  pl.BlockSpec((tf, d), lambda i, f: (f, 0))],
        out_specs=pl.BlockSpec((tm, d), lambda i, f: (i, 0)),
        scratch_shapes=[pltpu.VMEM((tm, d), BF16)],
        compiler_params=_params(("arbitrary", "arbitrary"), 48),
        name="dense_swiglu",
    )(x2, mod_l, g_pre.reshape(1, d), g_post.reshape(1, d), w1, w3, w2)


def _expert_ffn_kernel(te_ref, nv_ref, hs_ref, w1_ref, w3_ref, w2_ref, o_ref):
    i = pl.program_id(0)
    f = pl.program_id(1)

    @pl.when(f == 0)
    def _():
        o_ref[...] = jnp.zeros_like(o_ref)

    @pl.when(i < nv_ref[0])
    def _():
        _swiglu_accumulate(hs_ref[...].astype(BF16), w1_ref, w3_ref, w2_ref, o_ref)


def _expert_ffn(tile_expert, n_valid, hs, w1, w3, w2, *, tm):
    p, d = hs.shape
    dff = w1.shape[2]
    tf = FFN_COLS
    nf = dff // tf
    n_tiles = p // tm

    def row_map(i, f, te, nv):
        return (i, 0)

    def f_of(i, f, nv):
        return jnp.where(i < nv[0], f, nf - 1)

    grid_spec = pltpu.PrefetchScalarGridSpec(
        num_scalar_prefetch=2,
        grid=(n_tiles, nf),
        in_specs=[pl.BlockSpec((tm, d), row_map),
                  pl.BlockSpec((None, d, tf), lambda i, f, te, nv: (te[i], 0, f_of(i, f, nv))),
                  pl.BlockSpec((None, d, tf), lambda i, f, te, nv: (te[i], 0, f_of(i, f, nv))),
                  pl.BlockSpec((None, tf, d), lambda i, f, te, nv: (te[i], f_of(i, f, nv), 0))],
        out_specs=pl.BlockSpec((tm, d), row_map),
    )
    return pl.pallas_call(
        _expert_ffn_kernel,
        out_shape=jax.ShapeDtypeStruct((p, d), F32),
        grid_spec=grid_spec,
        compiler_params=_params(("arbitrary", "arbitrary"), 56),
        name="expert_swiglu",
    )(tile_expert, n_valid, hs, w1, w3, w2)


def _router_kernel(x_ref, mod_ref, g_ref, rw_ref, rb_ref, h_ref, route_ref, cnt_ref, carry, *, tm):
    i = pl.program_id(0)

    @pl.when(i == 0)
    def _():
        carry[...] = jnp.zeros_like(carry)

    h = _modulated_rms(x_ref[...], g_ref[...], mod_ref[0, 4:5, :], mod_ref[0, 3:4, :])
    h_ref[...] = h
    logits = jnp.dot(h.astype(BF16), rw_ref[...], preferred_element_type=F32) + rb_ref[...]
    lane = lax.broadcasted_iota(jnp.int32, (tm, LANE), 1)
    m1 = jnp.max(logits, axis=-1, keepdims=True)
    i1 = jnp.min(jnp.where(logits == m1, lane, LANE), axis=-1, keepdims=True)
    rest = jnp.where(lane == i1, -jnp.inf, logits)
    m2 = jnp.max(rest, axis=-1, keepdims=True)
    i2 = jnp.min(jnp.where(rest == m2, lane, LANE), axis=-1, keepdims=True)
    e = jnp.exp(m2 - m1)
    p1 = 1.0 / (1.0 + e)
    p2 = e / (1.0 + e)

    sel1 = lane == i1
    sel2 = lane == i2
    sel = jnp.logical_or(sel1, sel2).astype(F32)
    rows = lax.broadcasted_iota(jnp.int32, (tm, tm), 0)
    cols = lax.broadcasted_iota(jnp.int32, (tm, tm), 1)
    lower = (rows > cols).astype(BF16)
    rank = jnp.dot(lower, sel.astype(BF16), preferred_element_type=F32) + carry[...]
    r1 = jnp.sum(jnp.where(sel1, rank, 0.0), axis=-1, keepdims=True)
    r2 = jnp.sum(jnp.where(sel2, rank, 0.0), axis=-1, keepdims=True)
    carry[...] += jnp.sum(sel, axis=0, keepdims=True)
    cnt_ref[...] = carry[...]

    slab = jnp.where(lane == 0, i1.astype(F32), 0.0)
    slab = jnp.where(lane == 1, i2.astype(F32), slab)
    slab = jnp.where(lane == 2, r1, slab)
    slab = jnp.where(lane == 3, r2, slab)
    slab = jnp.where(lane == 4, p1, slab)
    slab = jnp.where(lane == 5, p2, slab)
    route_ref[...] = slab


def _router(x2, mod_l, g, rw_pad, rb_pad, *, seq):
    n, d = x2.shape
    tm = 512
    tiles_per_seq = seq // tm
    return pl.pallas_call(
        functools.partial(_router_kernel, tm=tm),
        out_shape=(jax.ShapeDtypeStruct((n, d), F32),
                   jax.ShapeDtypeStruct((n, LANE), F32),
                   jax.ShapeDtypeStruct((1, LANE), F32)),
        grid=(n // tm,),
        in_specs=[pl.BlockSpec((tm, d), lambda i: (i, 0)),
                  pl.BlockSpec((1, 6, d), lambda i: (i // tiles_per_seq, 0, 0)),
                  pl.BlockSpec((1, d), lambda i: (0, 0)),
                  pl.BlockSpec((d, LANE), lambda i: (0, 0)),
                  pl.BlockSpec((1, LANE), lambda i: (0, 0))],
        out_specs=(pl.BlockSpec((tm, d), lambda i: (i, 0)),
                   pl.BlockSpec((tm, LANE), lambda i: (i, 0)),
                   pl.BlockSpec((1, LANE), lambda i: (0, 0))),
        scratch_shapes=[pltpu.VMEM((1, LANE), F32)],
        compiler_params=_params(("arbitrary",), 40),
        name="router",
    )(x2, mod_l, g.reshape(1, d), rw_pad, rb_pad)


def _dispatch_kernel(pad_start_ref, pad_cnt_ref, pos_ref, h_ref, hs_ref, zero_scr, sem, zsem, *, tm):
    i = pl.program_id(0)

    def row_copy(r, k):
        return pltpu.make_async_copy(h_ref.at[pl.ds(r, 1), :],
                                     hs_ref.at[pl.ds(pos_ref[TOP_K * r + k], 1), :], sem)

    def issue(r, carry):
        for k in range(TOP_K):
            row_copy(r, k).start()
        return carry
    lax.fori_loop(0, tm, issue, 0)

    @pl.when(i == 0)
    def _():
        zero_scr[...] = jnp.zeros_like(zero_scr)
        for e in range(N_EXPERTS + 1):
            def zcopy(j, e=e):
                return pltpu.make_async_copy(zero_scr.at[pl.ds(0, 1), :],
                                             hs_ref.at[pl.ds(pad_start_ref[e] + j, 1), :], zsem)

            def zissue(j, carry):
                zcopy(j).start()
                return carry

            def zwait(j, carry):
                zcopy(j).wait()
                return carry
            lax.fori_loop(0, pad_cnt_ref[e], zissue, 0)
            lax.fori_loop(0, pad_cnt_ref[e], zwait, 0)

    for k in range(TOP_K):
        pltpu.make_async_copy(h_ref, hs_ref.at[pl.ds(0, tm), :], sem).wait()


def _dispatch(pad_start, pad_cnt, pos_flat, h2, *, p_rows):
    n, d = h2.shape
    tm = 512
    grid_spec = pltpu.PrefetchScalarGridSpec(
        num_scalar_prefetch=2,
        grid=(n // tm,),
        in_specs=[pl.BlockSpec((TOP_K * tm,), lambda i, ps, pc: (i,), memory_space=pltpu.SMEM),
                  pl.BlockSpec((tm, d), lambda i, ps, pc: (i, 0))],
        out_specs=pl.BlockSpec(memory_space=pl.ANY),
        scratch_shapes=[pltpu.VMEM((8, d), F32),
                        pltpu.SemaphoreType.DMA(()),
                        pltpu.SemaphoreType.DMA(())],
    )
    return pl.pallas_call(
        functools.partial(_dispatch_kernel, tm=tm),
        out_shape=jax.ShapeDtypeStruct((p_rows, d), F32),
        grid_spec=grid_spec,
        compiler_params=_params(("arbitrary",), 32),
        name="dispatch_rows",
    )(pad_start, pad_cnt, pos_flat, h2)


def _combine_kernel(pos_ref, os_ref, x_ref, route_ref, mod_ref, g_ref, o_ref, a_scr, b_scr, sem, *, tm):
    def row_copy(r, k):
        dst = a_scr if k == 0 else b_scr
        return pltpu.make_async_copy(os_ref.at[pl.ds(pos_ref[TOP_K * r + k], 1), :],
                                     dst.at[pl.ds(r, 1), :], sem)

    def issue(r, carry):
        for k in range(TOP_K):
            row_copy(r, k).start()
        return carry
    lax.fori_loop(0, tm, issue, 0)

    for dst in (a_scr, b_scr):
        pltpu.make_async_copy(os_ref.at[pl.ds(0, tm), :], dst, sem).wait()

    y = route_ref[:, 4:5] * a_scr[...] + route_ref[:, 5:6] * b_scr[...]
    o_ref[...] = x_ref[...] + mod_ref[0, 5:6, :] * _rms(y, g_ref[...])


def _combine(pos_flat, o_sorted, x2, route, mod_l, g, *, seq):
    n, d = x2.shape
    tm = 512
    tiles_per_seq = seq // tm
    return pl.pallas_call(
        functools.partial(_combine_kernel, tm=tm),
        out_shape=jax.ShapeDtypeStruct((n, d), F32),
        grid=(n // tm,),
        in_specs=[pl.BlockSpec((TOP_K * tm,), lambda i: (i,), memory_space=pltpu.SMEM),
                  pl.BlockSpec(memory_space=pl.ANY),
                  pl.BlockSpec((tm, d), lambda i: (i, 0)),
                  pl.BlockSpec((tm, LANE), lambda i: (i, 0)),
                  pl.BlockSpec((1, 6, d), lambda i: (i // tiles_per_seq, 0, 0)),
                  pl.BlockSpec((1, d), lambda i: (0, 0))],
        out_specs=pl.BlockSpec((tm, d), lambda i: (i, 0)),
        scratch_shapes=[pltpu.VMEM((tm, d), F32),
                        pltpu.VMEM((tm, d), F32),
                        pltpu.SemaphoreType.DMA(())],
        compiler_params=_params(("arbitrary",), 40),
        name="combine_rows",
    )(pos_flat, o_sorted, x2, route, mod_l, g.reshape(1, d))


def _moe_ffn(x2, mod_l, g_pre, g_post, router_w, router_b, w1, w3, w2, *, seq):
    n, d = x2.shape
    tm = EXPERT_ROWS
    n_tiles = (TOP_K * n) // tm + N_EXPERTS
    rw_pad = jnp.pad(router_w, ((0, 0), (0, LANE - N_EXPERTS))).astype(BF16)
    rb_pad = jnp.pad(router_b.reshape(1, -1), ((0, 0), (0, LANE - N_EXPERTS)),
                     constant_values=NEG_BIG)
    h2, route, cnt = _router(x2, mod_l, g_pre, rw_pad, rb_pad, seq=seq)

    counts = cnt[0, :N_EXPERTS].astype(jnp.int32)
    tiles_e = (counts + tm - 1) // tm
    tile_end = jnp.cumsum(tiles_e)
    group_start = (tile_end - tiles_e) * tm
    n_valid = tile_end[-1:]
    tile_ids = jnp.minimum(jnp.arange(n_tiles, dtype=jnp.int32), n_valid[0] - 1)
    tile_expert = jnp.sum(tile_ids[:, None] >= tile_end[None, :], axis=1).astype(jnp.int32)
    e_idx = route[:, 0:TOP_K].astype(jnp.int32)
    rank = route[:, TOP_K:2 * TOP_K].astype(jnp.int32)
    pos_flat = (group_start[e_idx] + rank).reshape(-1)

    pad_start = jnp.concatenate([group_start + counts, n_valid * tm])
    pad_cnt = jnp.concatenate([tiles_e * tm - counts, (n_tiles - n_valid) * tm])
    hs = _dispatch(pad_start, pad_cnt, pos_flat, h2, p_rows=n_tiles * tm)
    o_sorted = _expert_ffn(tile_expert, n_valid, hs, w1, w3, w2, tm=tm)
    return _combine(pos_flat, o_sorted, x2, route, mod_l, g_post, seq=seq)


def kernel(x, c, mod_w, mod_b, pre_mix_g, post_mix_g, pre_ffn_g, post_ffn_g, w_in, w_out, pool_w,
           pool_scale, sc_conv, cf_conv, cf_ln_g, cf_ln_b, ffn_w1, ffn_w3, ffn_w2, router_w,
           router_b, moe_w1, moe_w3, moe_w2):
    b, s, d = x.shape
    depth = mod_w.shape[0]
    mod = _modulation(c, mod_w, mod_b)
    x2 = x.reshape(b * s, d)
    for l in range(depth):
        mixed = _mix_in(x2, mod[l], pre_mix_g[l], w_in[l].astype(BF16), pool_w[l].astype(BF16),
                        pool_scale[l], sc_conv[l], cf_conv[l], cf_ln_g[l], cf_ln_b[l], seq=s)
        x2 = _mix_out(mixed, w_out[l].astype(BF16), x2, mod[l], post_mix_g[l], seq=s)
        j = l // 2
        if l % 2 == 0:
            x2 = _dense_ffn(x2, mod[l], pre_ffn_g[l], post_ffn_g[l], ffn_w1[j].astype(BF16),
                            ffn_w3[j].astype(BF16), ffn_w2[j].astype(BF16), seq=s)
        else:
            x2 = _moe_ffn(x2, mod[l], pre_ffn_g[l], post_ffn_g[l], router_w[j], router_b[j],
                          moe_w1[j].astype(BF16), moe_w3[j].astype(BF16), moe_w2[j].astype(BF16),
                          seq=s)
    return x2.reshape(b, s, d)
```

```python
import functools

import jax
import jax.numpy as jnp
from jax import lax
from jax.experimental import pallas as pl
from jax.experimental.pallas import tpu as pltpu

F32 = jnp.float32
BF16 = jnp.bfloat16

EPS = 1e-6
LANE = 128
POOL_WINDOWS = (2, 4, 8, 16)
POOL_GROUP = 128
POOL_WIDTH = POOL_GROUP * len(POOL_WINDOWS)
SC_KERNEL = 3
CF_KERNEL = 31
N_EXPERTS = 8
TOP_K = 2

POOL_HIST = 16
SC_HIST = 8
CF_HIST = 32

NEG_BIG = -1e30

DENSE_FFN_ROWS = 512
EXPERT_ROWS = 1024
FFN_COLS = 512
MIX_ROWS = 256
MIX_ROW_CHUNK = 64
PROJ_PIECE = 256
ROW_DMA_UNROLL = 8
CAST_SPLIT = 8

MIB = 1024 * 1024


def _params(sem, vmem_mib):
    return pltpu.CompilerParams(dimension_semantics=sem, vmem_limit_bytes=vmem_mib * MIB)


def _modulated_rms(x, g, scale, shift):
    ms = jnp.mean(x * x, axis=-1, keepdims=True)
    return (x * lax.rsqrt(ms + EPS) * g) * (1.0 + scale) + shift


def _rms(y, g):
    ms = jnp.mean(y * y, axis=-1, keepdims=True)
    return y * lax.rsqrt(ms + EPS) * g


def _mod_kernel(c_ref, w_ref, b_ref, o_ref):
    c = c_ref[...]
    ca = (c * jax.nn.sigmoid(c)).astype(BF16)
    o_ref[0] = jnp.dot(ca, w_ref[0].astype(BF16), preferred_element_type=F32) + b_ref[0]


def _modulation(c, mod_w, mod_b):
    depth, d, n6 = mod_w.shape
    b = c.shape[0]
    tn = 1024
    c8 = jnp.pad(c, ((0, 8 - b), (0, 0)))
    out = pl.pallas_call(
        _mod_kernel,
        out_shape=jax.ShapeDtypeStruct((depth, 8, n6), F32),
        grid=(depth, n6 // tn),
        in_specs=[pl.BlockSpec((8, d), lambda l, j: (0, 0)),
                  pl.BlockSpec((1, d, tn), lambda l, j: (l, 0, j)),
                  pl.BlockSpec((1, 1, tn), lambda l, j: (l, 0, j))],
        out_specs=pl.BlockSpec((1, 8, tn), lambda l, j: (l, 0, j)),
        compiler_params=_params(("arbitrary", "arbitrary"), 40),
        name="adaln_mod",
    )(c8, mod_w, mod_b.reshape(depth, 1, n6))
    return out[:, :b].reshape(depth, b, 6, d)


def _mix_in_kernel(x_ref, mod_ref, g_ref, win_ref, poolw_ref, pscale_ref, scw_ref, cfw_ref,
                   lng_ref, lnb_ref, o_ref, p_ref, hb_ref, vbuf, sbuf, ubuf, bbuf, cbuf, *, tiles_per_seq,
                   n_tiles, tm, rc, sc_width, cf_width):
    i = pl.program_id(0)
    o_sc = POOL_WIDTH
    o_c = o_sc + sc_width
    o_h = o_c + sc_width
    o_a = o_h + sc_width
    o_g = o_a + cf_width

    @pl.when(i == 0)
    def _():
        vbuf[...] = jnp.zeros_like(vbuf)
        sbuf[...] = jnp.zeros_like(sbuf)
        ubuf[...] = jnp.zeros_like(ubuf)
        bbuf[...] = jnp.zeros_like(bbuf)

    staged_in_seq = (jnp.maximum(i, 1) - 1) % tiles_per_seq
    t1 = (staged_in_seq * tm + 1 + lax.broadcasted_iota(jnp.int32, (tm, 1), 0)).astype(F32)
    work = []

    def pool_item(g, w, r0, inv_den):
        lanes = slice(g * POOL_GROUP, (g + 1) * POOL_GROUP)
        cur = vbuf[POOL_HIST + r0:POOL_HIST + r0 + rc, lanes]
        acc = cur
        for j in range(1, w):
            acc = acc + vbuf[POOL_HIST + r0 - j:POOL_HIST + r0 - j + rc, lanes]
        u = acc * inv_den[r0:r0 + rc, :] - cur
        z = jnp.dot(u.astype(BF16), poolw_ref[g], preferred_element_type=F32)
        o_ref[r0:r0 + rc, lanes] = (z * pscale_ref[:, lanes]).astype(o_ref.dtype)

    for g, w in enumerate(POOL_WINDOWS):
        inv_den = 1.0 / jnp.minimum(t1, float(w))
        for r0 in range(0, tm, rc):
            work.append((4 * w, functools.partial(pool_item, g, w, r0, inv_den)))

    def sc_item(c0, r0):
        lanes = slice(c0, c0 + LANE)
        acc = None
        for k in range(SC_KERNEL):
            s0 = SC_HIST + r0 - (SC_KERNEL - 1) + k
            term = scw_ref[k:k + 1, lanes] * sbuf[s0:s0 + rc, lanes]
            acc = term if acc is None else acc + term
        y = bbuf[r0:r0 + rc, lanes] * acc
        o_ref[r0:r0 + rc, POOL_WIDTH + c0:POOL_WIDTH + c0 + LANE] = y.astype(o_ref.dtype)

    for c0 in range(0, sc_width, LANE):
        for r0 in range(0, tm, rc):
            work.append((4 * SC_KERNEL, functools.partial(sc_item, c0, r0)))

    def cf_item(c0, r0):
        lanes = slice(c0, c0 + LANE)
        base = CF_HIST + r0
        acc = None
        for r in range(8):
            lo = base - (8 if r else 0)
            rows = rc + (8 if r else 0)
            part = None
            for q in range((CF_KERNEL - 1 - r) // 8 + 1):
                k = CF_KERNEL - 1 - 8 * q - r
                term = cfw_ref[k:k + 1, lanes] * ubuf[lo - 8 * q:lo - 8 * q + rows, lanes]
                part = term if part is None else part + term
            if r:
                part = part[8 - r:8 - r + rc, :]
            acc = part if acc is None else acc + part
        cbuf[r0:r0 + rc, lanes] = acc

    for c0 in range(0, cf_width, LANE):
        for r0 in range(0, tm, rc):
            work.append((3 * CF_KERNEL, functools.partial(cf_item, c0, r0)))

    h = _modulated_rms(x_ref[...], g_ref[...], mod_ref[0, 1:2, :], mod_ref[0, 0:1, :])
    hb_ref[...] = h.astype(BF16)
    d_in = p_ref.shape[1]
    pieces = range(0, d_in, PROJ_PIECE)
    per_piece = sum(cost for cost, _ in work) / len(pieces)
    done, spent = 0, 0.0
    for j, c0 in enumerate(pieces):
        p_ref[:, c0:c0 + PROJ_PIECE] = jnp.dot(hb_ref[...], win_ref[:, c0:c0 + PROJ_PIECE],
                                               preferred_element_type=F32)
        while done < len(work) and spent < (j + 1) * per_piece:
            cost, emit = work[done]
            emit()
            spent += cost
            done += 1

    u = cbuf[...]
    mu = jnp.mean(u, axis=-1, keepdims=True)
    d = u - mu
    var = jnp.mean(d * d, axis=-1, keepdims=True)
    ln = d * lax.rsqrt(var + EPS) * lng_ref[...] + lnb_ref[...]
    o_ref[:, POOL_WIDTH + sc_width:] = (ln * jax.nn.sigmoid(ln)).astype(o_ref.dtype)

    keep = (jnp.minimum(i, n_tiles - 1) % tiles_per_seq) != 0
    vbuf[0:POOL_HIST, :] = jnp.where(keep, vbuf[tm:tm + POOL_HIST, :], 0.0)
    sbuf[0:SC_HIST, :] = jnp.where(keep, sbuf[tm:tm + SC_HIST, :], 0.0)
    ubuf[0:CF_HIST, :] = jnp.where(keep, ubuf[tm:tm + CF_HIST, :], 0.0)
    vbuf[POOL_HIST:POOL_HIST + tm, :] = p_ref[:, 0:POOL_WIDTH]
    bbuf[...] = p_ref[:, o_sc:o_sc + sc_width]
    sbuf[SC_HIST:SC_HIST + tm, :] = p_ref[:, o_c:o_c + sc_width] * p_ref[:, o_h:o_h + sc_width]
    ubuf[CF_HIST:CF_HIST + tm, :] = (p_ref[:, o_a:o_a + cf_width]
                                     * jax.nn.sigmoid(p_ref[:, o_g:o_g + cf_width]))


def _mix_in(x2, mod_l, g, w_in, pool_w, pool_scale, sc_conv, cf_conv, ln_g, ln_b, *, seq):
    n, d = x2.shape
    d_in = w_in.shape[1]
    sc_width = sc_conv.shape[1]
    cf_width = cf_conv.shape[1]
    tm = MIX_ROWS
    n_tiles = n // tm
    tiles_per_seq = seq // tm
    const = lambda i: (0, 0)
    cur = lambda i: jnp.minimum(i, n_tiles - 1)
    return pl.pallas_call(
        functools.partial(_mix_in_kernel, tiles_per_seq=tiles_per_seq, n_tiles=n_tiles, tm=tm,
                          rc=MIX_ROW_CHUNK, sc_width=sc_width, cf_width=cf_width),
        out_shape=jax.ShapeDtypeStruct((n, d), BF16),
        grid=(n_tiles + 1,),
        in_specs=[pl.BlockSpec((tm, d), lambda i: (cur(i), 0)),
                  pl.BlockSpec((1, 6, d), lambda i: (cur(i) // tiles_per_seq, 0, 0)),
                  pl.BlockSpec((1, d), const),
                  pl.BlockSpec((d, d_in), const, pipeline_mode=pl.Buffered(1)),
                  pl.BlockSpec(pool_w.shape, lambda i: (0, 0, 0)),
                  pl.BlockSpec((1, POOL_WIDTH), const),
                  pl.BlockSpec(sc_conv.shape, const),
                  pl.BlockSpec(cf_conv.shape, const),
                  pl.BlockSpec((1, cf_width), const),
                  pl.BlockSpec((1, cf_width), const)],
        out_specs=pl.BlockSpec((tm, d), lambda i: (jnp.maximum(i, 1) - 1, 0)),
        scratch_shapes=[pltpu.VMEM((tm, d_in), F32),
                        pltpu.VMEM((tm, d), BF16),
                        pltpu.VMEM((POOL_HIST + tm, POOL_WIDTH), F32),
                        pltpu.VMEM((SC_HIST + tm, sc_width), F32),
                        pltpu.VMEM((CF_HIST + tm, cf_width), F32),
                        pltpu.VMEM((tm, sc_width), F32),
                        pltpu.VMEM((tm, cf_width), F32)],
        compiler_params=_params(("arbitrary",), 48),
        name="mix_in",
    )(x2, mod_l, g.reshape(1, d), w_in, pool_w, pool_scale.reshape(1, -1), sc_conv, cf_conv,
      ln_g.reshape(1, -1), ln_b.reshape(1, -1))


def _mix_out_kernel(m_ref, w_ref, x_ref, mod_ref, g_ref, o_ref):
    y = jnp.dot(m_ref[...], w_ref[...], preferred_element_type=F32)
    o_ref[...] = x_ref[...] + mod_ref[0, 2:3, :] * _rms(y, g_ref[...])


def _mix_out(mixed, w_out, x2, mod_l, g, *, seq):
    n, d = x2.shape
    tm = 512
    tiles_per_seq = seq // tm
    return pl.pallas_call(
        _mix_out_kernel,
        out_shape=jax.ShapeDtypeStruct((n, d), F32),
        grid=(n // tm,),
        in_specs=[pl.BlockSpec((tm, d), lambda i: (i, 0)),
                  pl.BlockSpec((d, d), lambda i: (0, 0), pipeline_mode=pl.Buffered(1)),
                  pl.BlockSpec((tm, d), lambda i: (i, 0)),
                  pl.BlockSpec((1, 6, d), lambda i: (i // tiles_per_seq, 0, 0)),
                  pl.BlockSpec((1, d), lambda i: (0, 0))],
        out_specs=pl.BlockSpec((tm, d), lambda i: (i, 0)),
        compiler_params=_params(("arbitrary",), 48),
        name="mix_out",
    )(mixed, w_out, x2, mod_l, g.reshape(1, d))


def _swiglu_accumulate(h, w1_ref, w3_ref, w2_ref, o_ref):
    a1 = jnp.dot(h, w1_ref[...], preferred_element_type=F32)
    a3 = jnp.dot(h, w3_ref[...], preferred_element_type=F32)
    act = (a1 * jax.nn.sigmoid(a1) * a3).astype(BF16)
    o_ref[...] += jnp.dot(act, w2_ref[...], preferred_element_type=F32)


def _dense_ffn_kernel(*refs, nf, n_cast):
    x_ref, mod_ref, gpre_ref, gpost_ref, w1_ref, w3_ref, w2_ref = refs[:7]
    cast_in = refs[7:7 + n_cast]
    o_ref = refs[7 + n_cast]
    cast_out = refs[8 + n_cast:8 + 2 * n_cast]
    h_scr = refs[8 + 2 * n_cast]
    f = pl.program_id(1)

    @pl.when(f == 0)
    def _():
        h = _modulated_rms(x_ref[...], gpre_ref[...], mod_ref[0, 4:5, :], mod_ref[0, 3:4, :])
        h_scr[...] = h.astype(BF16)
        o_ref[...] = jnp.zeros_like(o_ref)

    _swiglu_accumulate(h_scr[...], w1_ref, w3_ref, w2_ref, o_ref)
    for src, dst in zip(cast_in, cast_out):
        dst[...] = src[...].astype(BF16)

    @pl.when(f == nf - 1)
    def _():
        o_ref[...] = x_ref[...] + mod_ref[0, 5:6, :] * _rms(o_ref[...], gpost_ref[...])


def _dense_ffn(x2, mod_l, g_pre, g_post, w1, w3, w2, cast_weights, *, seq):
    n, d = x2.shape
    dff = w1.shape[1]
    tm, tf = DENSE_FFN_ROWS, FFN_COLS
    nf = dff // tf
    n_i = n // tm
    tiles_per_seq = seq // tm
    assert nf >= CAST_SPLIT or not cast_weights
    views, cast_specs = [], []
    for w in cast_weights:
        v = w.reshape(-1, w.shape[-1])
        rows, cols = v.shape
        assert rows % (n_i * 16) == 0 and cols % (CAST_SPLIT * LANE) == 0, v.shape
        views.append(v)
        cast_specs.append(pl.BlockSpec((rows // n_i, cols // CAST_SPLIT),
                                       lambda i, f: (i, jnp.minimum(f, CAST_SPLIT - 1))))
    outs = pl.pallas_call(
        functools.partial(_dense_ffn_kernel, nf=nf, n_cast=len(views)),
        out_shape=[jax.ShapeDtypeStruct((n, d), F32)]
                  + [jax.ShapeDtypeStruct(v.shape, BF16) for v in views],
        grid=(n_i, nf),
        in_specs=[pl.BlockSpec((tm, d), lambda i, f: (i, 0)),
                  pl.BlockSpec((1, 6, d), lambda i, f: (i // tiles_per_seq, 0, 0)),
                  pl.BlockSpec((1, d), lambda i, f: (0, 0)),
                  pl.BlockSpec((1, d), lambda i, f: (0, 0)),
                  pl.BlockSpec((d, tf), lambda i, f: (0, f)),
                  pl.BlockSpec((d, tf), lambda i, f: (0, f)),
                  pl.BlockSpec((tf, d), lambda i, f: (f, 0))] + cast_specs,
        out_specs=[pl.BlockSpec((tm, d), lambda i, f: (i, 0))] + cast_specs,
        scratch_shapes=[pltpu.VMEM((tm, d), BF16)],
        compiler_params=_params(("arbitrary", "arbitrary"), 56),
        name="dense_swiglu",
    )(x2, mod_l, g_pre.reshape(1, d), g_post.reshape(1, d), w1, w3, w2, *views)
    return outs[0], [o.reshape(w.shape) for o, w in zip(outs[1:], cast_weights)]


def _expert_ffn_kernel(te_ref, nv_ref, hs_ref, w1_ref, w3_ref, w2_ref, o_ref):
    i = pl.program_id(0)
    f = pl.program_id(1)

    @pl.when(f == 0)
    def _():
        o_ref[...] = jnp.zeros_like(o_ref)

    @pl.when(i < nv_ref[0])
    def _():
        _swiglu_accumulate(hs_ref[...].astype(BF16), w1_ref, w3_ref, w2_ref, o_ref)


def _expert_ffn(tile_expert, n_valid, hs, w1, w3, w2, *, tm):
    p, d = hs.shape
    dff = w1.shape[2]
    tf = FFN_COLS
    nf = dff // tf
    n_tiles = p // tm

    def row_map(i, f, te, nv):
        return (i, 0)

    def f_of(i, f, nv):
        return jnp.where(i < nv[0], f, nf - 1)

    grid_spec = pltpu.PrefetchScalarGridSpec(
        num_scalar_prefetch=2,
        grid=(n_tiles, nf),
        in_specs=[pl.BlockSpec((tm, d), row_map),
                  pl.BlockSpec((None, d, tf), lambda i, f, te, nv: (te[i], 0, f_of(i, f, nv))),
                  pl.BlockSpec((None, d, tf), lambda i, f, te, nv: (te[i], 0, f_of(i, f, nv))),
                  pl.BlockSpec((None, tf, d), lambda i, f, te, nv: (te[i], f_of(i, f, nv), 0))],
        out_specs=pl.BlockSpec((tm, d), row_map),
    )
    return pl.pallas_call(
        _expert_ffn_kernel,
        out_shape=jax.ShapeDtypeStruct((p, d), F32),
        grid_spec=grid_spec,
        compiler_params=_params(("arbitrary", "arbitrary"), 56),
        name="expert_swiglu",
    )(tile_expert, n_valid, hs, w1, w3, w2)


def _router_kernel(x_ref, mod_ref, g_ref, rw_ref, rb_ref, h_ref, route_ref, cnt_ref, carry, *, tm):
    i = pl.program_id(0)

    @pl.when(i == 0)
    def _():
        carry[...] = jnp.zeros_like(carry)

    h = _modulated_rms(x_ref[...], g_ref[...], mod_ref[0, 4:5, :], mod_ref[0, 3:4, :])
    h_ref[...] = h
    logits = jnp.dot(h.astype(BF16), rw_ref[...], preferred_element_type=F32) + rb_ref[...]
    lane = lax.broadcasted_iota(jnp.int32, (tm, LANE), 1)
    m1 = jnp.max(logits, axis=-1, keepdims=True)
    i1 = jnp.min(jnp.where(logits == m1, lane, LANE), axis=-1, keepdims=True)
    rest = jnp.where(lane == i1, -jnp.inf, logits)
    m2 = jnp.max(rest, axis=-1, keepdims=True)
    i2 = jnp.min(jnp.where(rest == m2, lane, LANE), axis=-1, keepdims=True)
    e = jnp.exp(m2 - m1)
    p1 = 1.0 / (1.0 + e)
    p2 = e / (1.0 + e)

    sel1 = lane == i1
    sel2 = lane == i2
    sel = jnp.logical_or(sel1, sel2).astype(F32)
    rows = lax.broadcasted_iota(jnp.int32, (tm, tm), 0)
    cols = lax.broadcasted_iota(jnp.int32, (tm, tm), 1)
    lower = (rows > cols).astype(BF16)
    rank = jnp.dot(lower, sel.astype(BF16), preferred_element_type=F32) + carry[...]
    r1 = jnp.sum(jnp.where(sel1, rank, 0.0), axis=-1, keepdims=True)
    r2 = jnp.sum(jnp.where(sel2, rank, 0.0), axis=-1, keepdims=True)
    carry[...] += jnp.sum(sel, axis=0, keepdims=True)
    cnt_ref[...] = carry[...]

    slab = jnp.where(lane == 0, i1.astype(F32), 0.0)
    slab = jnp.where(lane == 1, i2.astype(F32), slab)
    slab = jnp.where(lane == 2, r1, slab)
    slab = jnp.where(lane == 3, r2, slab)
    slab = jnp.where(lane == 4, p1, slab)
    slab = jnp.where(lane == 5, p2, slab)
    route_ref[...] = slab


def _router(x2, mod_l, g, rw_pad, rb_pad, *, seq):
    n, d = x2.shape
    tm = 512
    tiles_per_seq = seq // tm
    return pl.pallas_call(
        functools.partial(_router_kernel, tm=tm),
        out_shape=(jax.ShapeDtypeStruct((n, d), F32),
                   jax.ShapeDtypeStruct((n, LANE), F32),
                   jax.ShapeDtypeStruct((1, LANE), F32)),
        grid=(n // tm,),
        in_specs=[pl.BlockSpec((tm, d), lambda i: (i, 0)),
                  pl.BlockSpec((1, 6, d), lambda i: (i // tiles_per_seq, 0, 0)),
                  pl.BlockSpec((1, d), lambda i: (0, 0)),
                  pl.BlockSpec((d, LANE), lambda i: (0, 0)),
                  pl.BlockSpec((1, LANE), lambda i: (0, 0))],
        out_specs=(pl.BlockSpec((tm, d), lambda i: (i, 0)),
                   pl.BlockSpec((tm, LANE), lambda i: (i, 0)),
                   pl.BlockSpec((1, LANE), lambda i: (0, 0))),
        scratch_shapes=[pltpu.VMEM((1, LANE), F32)],
        compiler_params=_params(("arbitrary",), 40),
        name="router",
    )(x2, mod_l, g.reshape(1, d), rw_pad, rb_pad)


def _dispatch_kernel(pad_start_ref, pad_cnt_ref, pos_ref, h_ref, hs_ref, zero_scr, sem, zsem, *, tm):
    i = pl.program_id(0)

    def row_copy(r, k):
        return pltpu.make_async_copy(h_ref.at[pl.ds(r, 1), :],
                                     hs_ref.at[pl.ds(pos_ref[TOP_K * r + k], 1), :], sem)

    def issue(r, carry):
        for k in range(TOP_K):
            row_copy(r, k).start(priority=k)
        return carry
    lax.fori_loop(0, tm, issue, 0, unroll=ROW_DMA_UNROLL)

    @pl.when(i == 0)
    def _():
        zero_scr[...] = jnp.zeros_like(zero_scr)
        for e in range(N_EXPERTS + 1):
            def zcopy(j, e=e):
                return pltpu.make_async_copy(zero_scr.at[pl.ds(0, 1), :],
                                             hs_ref.at[pl.ds(pad_start_ref[e] + j, 1), :], zsem)

            def zissue(j, carry):
                zcopy(j).start()
                return carry

            def zwait(j, carry):
                zcopy(j).wait()
                return carry
            lax.fori_loop(0, pad_cnt_ref[e], zissue, 0)
            lax.fori_loop(0, pad_cnt_ref[e], zwait, 0)

    for k in range(TOP_K):
        pltpu.make_async_copy(h_ref, hs_ref.at[pl.ds(0, tm), :], sem).wait()


def _dispatch(pad_start, pad_cnt, pos_flat, h2, *, p_rows):
    n, d = h2.shape
    tm = 512
    grid_spec = pltpu.PrefetchScalarGridSpec(
        num_scalar_prefetch=2,
        grid=(n // tm,),
        in_specs=[pl.BlockSpec((TOP_K * tm,), lambda i, ps, pc: (i,), memory_space=pltpu.SMEM),
                  pl.BlockSpec((tm, d), lambda i, ps, pc: (i, 0))],
        out_specs=pl.BlockSpec(memory_space=pl.ANY),
        scratch_shapes=[pltpu.VMEM((8, d), F32),
                        pltpu.SemaphoreType.DMA(()),
                        pltpu.SemaphoreType.DMA(())],
    )
    return pl.pallas_call(
        functools.partial(_dispatch_kernel, tm=tm),
        out_shape=jax.ShapeDtypeStruct((p_rows, d), F32),
        grid_spec=grid_spec,
        compiler_params=_params(("arbitrary",), 32),
        name="dispatch_rows",
    )(pad_start, pad_cnt, pos_flat, h2)


def _combine_kernel(pos_ref, os_ref, x_ref, route_ref, mod_ref, g_ref, o_ref, a_scr, b_scr, sem, *, tm):
    def row_copy(r, k):
        dst = a_scr if k == 0 else b_scr
        return pltpu.make_async_copy(os_ref.at[pl.ds(pos_ref[TOP_K * r + k], 1), :],
                                     dst.at[pl.ds(r, 1), :], sem)

    def issue(r, carry):
        for k in range(TOP_K):
            row_copy(r, k).start(priority=k)
        return carry
    lax.fori_loop(0, tm, issue, 0, unroll=ROW_DMA_UNROLL)

    for dst in (a_scr, b_scr):
        pltpu.make_async_copy(os_ref.at[pl.ds(0, tm), :], dst, sem).wait()

    y = route_ref[:, 4:5] * a_scr[...] + route_ref[:, 5:6] * b_scr[...]
    o_ref[...] = x_ref[...] + mod_ref[0, 5:6, :] * _rms(y, g_ref[...])


def _combine(pos_flat, o_sorted, x2, route, mod_l, g, *, seq):
    n, d = x2.shape
    tm = 512
    tiles_per_seq = seq // tm
    return pl.pallas_call(
        functools.partial(_combine_kernel, tm=tm),
        out_shape=jax.ShapeDtypeStruct((n, d), F32),
        grid=(n // tm,),
        in_specs=[pl.BlockSpec((TOP_K * tm,), lambda i: (i,), memory_space=pltpu.SMEM),
                  pl.BlockSpec(memory_space=pl.ANY),
                  pl.BlockSpec((tm, d), lambda i: (i, 0)),
                  pl.BlockSpec((tm, LANE), lambda i: (i, 0)),
                  pl.BlockSpec((1, 6, d), lambda i: (i // tiles_per_seq, 0, 0)),
                  pl.BlockSpec((1, d), lambda i: (0, 0))],
        out_specs=pl.BlockSpec((tm, d), lambda i: (i, 0)),
        scratch_shapes=[pltpu.VMEM((tm, d), F32),
                        pltpu.VMEM((tm, d), F32),
                        pltpu.SemaphoreType.DMA(())],
        compiler_params=_params(("arbitrary",), 40),
        name="combine_rows",
    )(pos_flat, o_sorted, x2, route, mod_l, g.reshape(1, d))


def _moe_ffn(x2, mod_l, g_pre, g_post, router_w, router_b, w1, w3, w2, *, seq):
    n, d = x2.shape
    tm = EXPERT_ROWS
    n_tiles = (TOP_K * n) // tm + N_EXPERTS
    rw_pad = jnp.pad(router_w, ((0, 0), (0, LANE - N_EXPERTS))).astype(BF16)
    rb_pad = jnp.pad(router_b.reshape(1, -1), ((0, 0), (0, LANE - N_EXPERTS)),
                     constant_values=NEG_BIG)
    h2, route, cnt = _router(x2, mod_l, g_pre, rw_pad, rb_pad, seq=seq)

    counts = cnt[0, :N_EXPERTS].astype(jnp.int32)
    tiles_e = (counts + tm - 1) // tm
    tile_end = jnp.cumsum(tiles_e)
    group_start = (tile_end - tiles_e) * tm
    n_valid = tile_end[-1:]
    tile_ids = jnp.minimum(jnp.arange(n_tiles, dtype=jnp.int32), n_valid[0] - 1)
    tile_expert = jnp.sum(tile_ids[:, None] >= tile_end[None, :], axis=1).astype(jnp.int32)
    e_idx = route[:, 0:TOP_K].astype(jnp.int32)
    rank = route[:, TOP_K:2 * TOP_K].astype(jnp.int32)
    pos_flat = (group_start[e_idx] + rank).reshape(-1)

    pad_start = jnp.concatenate([group_start + counts, n_valid * tm])
    pad_cnt = jnp.concatenate([tiles_e * tm - counts, (n_tiles - n_valid) * tm])
    hs = _dispatch(pad_start, pad_cnt, pos_flat, h2, p_rows=n_tiles * tm)
    o_sorted = _expert_ffn(tile_expert, n_valid, hs, w1, w3, w2, tm=tm)
    return _combine(pos_flat, o_sorted, x2, route, mod_l, g_post, seq=seq)


def kernel(x, c, mod_w, mod_b, pre_mix_g, post_mix_g, pre_ffn_g, post_ffn_g, w_in, w_out, pool_w,
           pool_scale, sc_conv, cf_conv, cf_ln_g, cf_ln_b, ffn_w1, ffn_w3, ffn_w2, router_w,
           router_b, moe_w1, moe_w3, moe_w2):
    b, s, d = x.shape
    depth = mod_w.shape[0]
    mod = _modulation(c, mod_w, mod_b)
    x2 = x.reshape(b * s, d)
    moe_bf16 = {}
    for l in range(depth):
        mixed = _mix_in(x2, mod[l], pre_mix_g[l], w_in[l].astype(BF16), pool_w[l].astype(BF16),
                        pool_scale[l], sc_conv[l], cf_conv[l], cf_ln_g[l], cf_ln_b[l], seq=s)
        x2 = _mix_out(mixed, w_out[l].astype(BF16), x2, mod[l], post_mix_g[l], seq=s)
        j = l // 2
        if l % 2 == 0:
            nxt = [moe_w1[j], moe_w3[j], moe_w2[j]] if l + 1 < depth else []
            x2, cast = _dense_ffn(x2, mod[l], pre_ffn_g[l], post_ffn_g[l], ffn_w1[j].astype(BF16),
                                  ffn_w3[j].astype(BF16), ffn_w2[j].astype(BF16), nxt, seq=s)
            if cast:
                moe_bf16[j] = cast
        else:
            w1b, w3b, w2b = moe_bf16.get(j) or [w[j].astype(BF16) for w in (moe_w1, moe_w3, moe_w2)]
            x2 = _moe_ffn(x2, mod[l], pre_ffn_g[l], post_ffn_g[l], router_w[j], router_b[j],
                          w1b, w3b, w2b, seq=s)
    return x2.reshape(b, s, d)
```

```python
import functools

import jax
import jax.numpy as jnp
from jax import lax
from jax.experimental import pallas as pl
from jax.experimental.pallas import tpu as pltpu

F32 = jnp.float32
BF16 = jnp.bfloat16

EPS = 1e-6
LANE = 128
POOL_WINDOWS = (2, 4, 8, 16)
POOL_GROUP = 128
POOL_WIDTH = POOL_GROUP * len(POOL_WINDOWS)
SC_KERNEL = 3
CF_KERNEL = 31
N_EXPERTS = 8
TOP_K = 2

POOL_HIST = 16
SC_HIST = 8
CF_HIST = 32

NEG_BIG = -1e30

DENSE_FFN_ROWS = 512
EXPERT_ROWS = 1024
FFN_COLS = 512
MIX_ROWS = 256
MIX_ROW_CHUNK = 64
PROJ_PIECE = 256
MIX_LANES_PER_TRIP = 3
ROW_DMA_UNROLL = 8
CAST_SPLIT = 8

MIB = 1024 * 1024


def _params(sem, vmem_mib):
    return pltpu.CompilerParams(dimension_semantics=sem, vmem_limit_bytes=vmem_mib * MIB)


def _modulated_rms(x, g, scale, shift):
    ms = jnp.mean(x * x, axis=-1, keepdims=True)
    return (x * lax.rsqrt(ms + EPS) * g) * (1.0 + scale) + shift


def _rms(y, g):
    ms = jnp.mean(y * y, axis=-1, keepdims=True)
    return y * lax.rsqrt(ms + EPS) * g


def _mod_kernel(c_ref, w_ref, b_ref, o_ref):
    c = c_ref[...]
    ca = (c * jax.nn.sigmoid(c)).astype(BF16)
    o_ref[0] = jnp.dot(ca, w_ref[0].astype(BF16), preferred_element_type=F32) + b_ref[0]


def _modulation(c, mod_w, mod_b):
    depth, d, n6 = mod_w.shape
    b = c.shape[0]
    tn = 1024
    c8 = jnp.pad(c, ((0, 8 - b), (0, 0)))
    out = pl.pallas_call(
        _mod_kernel,
        out_shape=jax.ShapeDtypeStruct((depth, 8, n6), F32),
        grid=(depth, n6 // tn),
        in_specs=[pl.BlockSpec((8, d), lambda l, j: (0, 0)),
                  pl.BlockSpec((1, d, tn), lambda l, j: (l, 0, j)),
                  pl.BlockSpec((1, 1, tn), lambda l, j: (l, 0, j))],
        out_specs=pl.BlockSpec((1, 8, tn), lambda l, j: (l, 0, j)),
        compiler_params=_params(("arbitrary", "arbitrary"), 40),
        name="adaln_mod",
    )(c8, mod_w, mod_b.reshape(depth, 1, n6))
    return out[:, :b].reshape(depth, b, 6, d)


def _mix_in_kernel(x_ref, mod_ref, g_ref, win_ref, poolw_ref, pscale_ref, scw_ref, cfw_ref,
                   lng_ref, lnb_ref, o_ref, p_ref, hb_ref, vbuf, sbuf, ubuf, bbuf, cbuf, *, tiles_per_seq,
                   n_tiles, tm, rc, sc_width, cf_width):
    i = pl.program_id(0)
    o_sc = POOL_WIDTH
    o_c = o_sc + sc_width
    o_h = o_c + sc_width
    o_a = o_h + sc_width
    o_g = o_a + cf_width

    @pl.when(i == 0)
    def _():
        vbuf[...] = jnp.zeros_like(vbuf)
        sbuf[...] = jnp.zeros_like(sbuf)
        ubuf[...] = jnp.zeros_like(ubuf)
        bbuf[...] = jnp.zeros_like(bbuf)

    staged_in_seq = (jnp.maximum(i, 1) - 1) % tiles_per_seq

    def window(buf, hist, r0, back, n, lanes):
        return buf[pl.ds(r0 + (hist - back), n), lanes]

    def pool_rows(r0):
        t1 = (staged_in_seq * tm + r0 + 1 + lax.broadcasted_iota(jnp.int32, (rc, 1), 0)).astype(F32)
        for g, w in enumerate(POOL_WINDOWS):
            lanes = slice(g * POOL_GROUP, (g + 1) * POOL_GROUP)
            win = window(vbuf, POOL_HIST, r0, POOL_HIST, rc + POOL_HIST, lanes)
            cur = win[POOL_HIST:, :]
            acc = cur
            for j in range(1, w):
                acc = acc + win[POOL_HIST - j:POOL_HIST - j + rc, :]
            u = acc * (1.0 / jnp.minimum(t1, float(w))) - cur
            z = jnp.dot(u.astype(BF16), poolw_ref[g], preferred_element_type=F32)
            o_ref[pl.ds(r0, rc), lanes] = (z * pscale_ref[:, lanes]).astype(o_ref.dtype)

    def sc_chunk(c0, r0):
        lanes = pl.ds(c0, LANE)
        win = window(sbuf, SC_HIST, r0, SC_HIST, rc + SC_HIST, lanes)
        acc = None
        for k in range(SC_KERNEL):
            s0 = SC_HIST - (SC_KERNEL - 1) + k
            term = scw_ref[k:k + 1, lanes] * win[s0:s0 + rc, :]
            acc = term if acc is None else acc + term
        y = bbuf[pl.ds(r0, rc), lanes] * acc
        o_ref[pl.ds(r0, rc), pl.ds(POOL_WIDTH + c0, LANE)] = y.astype(o_ref.dtype)

    def cf_chunk(c0, r0):
        lanes = pl.ds(c0, LANE)
        acc = None
        for r in range(8):
            back = 8 if r else 0
            part = None
            for q in range((CF_KERNEL - 1 - r) // 8 + 1):
                k = CF_KERNEL - 1 - 8 * q - r
                term = cfw_ref[k:k + 1, lanes] * window(ubuf, CF_HIST, r0, back + 8 * q,
                                                        rc + back, lanes)
                part = term if part is None else part + term
            if r:
                part = part[8 - r:8 - r + rc, :]
            acc = part if acc is None else acc + part
        cbuf[pl.ds(r0, rc), lanes] = acc

    sc_in_trips = sc_width == cf_width
    for r0 in range(0, tm, rc):
        pool_rows(r0)
        if not sc_in_trips:
            for c0 in range(0, sc_width, LANE):
                sc_chunk(c0, r0)

    h = _modulated_rms(x_ref[...], g_ref[...], mod_ref[0, 1:2, :], mod_ref[0, 0:1, :])
    hb_ref[...] = h.astype(BF16)

    def project(pc):
        cols = pl.ds(pc, PROJ_PIECE)
        p_ref[:, cols] = jnp.dot(hb_ref[...], win_ref[:, cols], preferred_element_type=F32)

    n_pieces = p_ref.shape[1] // PROJ_PIECE
    n_trips = cf_width // (MIX_LANES_PER_TRIP * LANE)
    per_trip = n_pieces // n_trips
    for j in range(per_trip * n_trips, n_pieces):
        project(j * PROJ_PIECE)

    def trip(it, carry):
        for m in range(per_trip):
            project(pl.multiple_of((it * per_trip + m) * PROJ_PIECE, PROJ_PIECE))
        for m in range(MIX_LANES_PER_TRIP):
            c0 = pl.multiple_of((it * MIX_LANES_PER_TRIP + m) * LANE, LANE)
            for r0 in range(0, tm, rc):
                cf_chunk(c0, r0)
                if sc_in_trips:
                    sc_chunk(c0, r0)
        return carry
    lax.fori_loop(0, n_trips, trip, 0)

    u = cbuf[...]
    mu = jnp.mean(u, axis=-1, keepdims=True)
    d = u - mu
    var = jnp.mean(d * d, axis=-1, keepdims=True)
    ln = d * lax.rsqrt(var + EPS) * lng_ref[...] + lnb_ref[...]
    o_ref[:, POOL_WIDTH + sc_width:] = (ln * jax.nn.sigmoid(ln)).astype(o_ref.dtype)

    keep = (jnp.minimum(i, n_tiles - 1) % tiles_per_seq) != 0
    vbuf[0:POOL_HIST, :] = jnp.where(keep, vbuf[tm:tm + POOL_HIST, :], 0.0)
    sbuf[0:SC_HIST, :] = jnp.where(keep, sbuf[tm:tm + SC_HIST, :], 0.0)
    ubuf[0:CF_HIST, :] = jnp.where(keep, ubuf[tm:tm + CF_HIST, :], 0.0)
    vbuf[POOL_HIST:POOL_HIST + tm, :] = p_ref[:, 0:POOL_WIDTH]
    bbuf[...] = p_ref[:, o_sc:o_sc + sc_width]
    sbuf[SC_HIST:SC_HIST + tm, :] = p_ref[:, o_c:o_c + sc_width] * p_ref[:, o_h:o_h + sc_width]
    ubuf[CF_HIST:CF_HIST + tm, :] = (p_ref[:, o_a:o_a + cf_width]
                                     * jax.nn.sigmoid(p_ref[:, o_g:o_g + cf_width]))


def _mix_in(x2, mod_l, g, w_in, pool_w, pool_scale, sc_conv, cf_conv, ln_g, ln_b, *, seq):
    n, d = x2.shape
    d_in = w_in.shape[1]
    sc_width = sc_conv.shape[1]
    cf_width = cf_conv.shape[1]
    tm = MIX_ROWS
    n_tiles = n // tm
    tiles_per_seq = seq // tm
    const = lambda i: (0, 0)
    cur = lambda i: jnp.minimum(i, n_tiles - 1)
    assert d_in % PROJ_PIECE == 0 and cf_width % (MIX_LANES_PER_TRIP * LANE) == 0
    return pl.pallas_call(
        functools.partial(_mix_in_kernel, tiles_per_seq=tiles_per_seq, n_tiles=n_tiles, tm=tm,
                          rc=MIX_ROW_CHUNK, sc_width=sc_width, cf_width=cf_width),
        out_shape=jax.ShapeDtypeStruct((n, d), BF16),
        grid=(n_tiles + 1,),
        in_specs=[pl.BlockSpec((tm, d), lambda i: (cur(i), 0)),
                  pl.BlockSpec((1, 6, d), lambda i: (cur(i) // tiles_per_seq, 0, 0)),
                  pl.BlockSpec((1, d), const),
                  pl.BlockSpec((d, d_in), const, pipeline_mode=pl.Buffered(1)),
                  pl.BlockSpec(pool_w.shape, lambda i: (0, 0, 0)),
                  pl.BlockSpec((1, POOL_WIDTH), const),
                  pl.BlockSpec(sc_conv.shape, const),
                  pl.BlockSpec(cf_conv.shape, const),
                  pl.BlockSpec((1, cf_width), const),
                  pl.BlockSpec((1, cf_width), const)],
        out_specs=pl.BlockSpec((tm, d), lambda i: (jnp.maximum(i, 1) - 1, 0)),
        scratch_shapes=[pltpu.VMEM((tm, d_in), F32),
                        pltpu.VMEM((tm, d), BF16),
                        pltpu.VMEM((POOL_HIST + tm, POOL_WIDTH), F32),
                        pltpu.VMEM((SC_HIST + tm, sc_width), F32),
                        pltpu.VMEM((CF_HIST + tm, cf_width), F32),
                        pltpu.VMEM((tm, sc_width), F32),
                        pltpu.VMEM((tm, cf_width), F32)],
        compiler_params=_params(("arbitrary",), 48),
        name="mix_in",
    )(x2, mod_l, g.reshape(1, d), w_in, pool_w, pool_scale.reshape(1, -1), sc_conv, cf_conv,
      ln_g.reshape(1, -1), ln_b.reshape(1, -1))


def _mix_out_kernel(m_ref, w_ref, x_ref, mod_ref, g_ref, o_ref):
    y = jnp.dot(m_ref[...], w_ref[...], preferred_element_type=F32)
    o_ref[...] = x_ref[...] + mod_ref[0, 2:3, :] * _rms(y, g_ref[...])


def _mix_out(mixed, w_out, x2, mod_l, g, *, seq):
    n, d = x2.shape
    tm = 512
    tiles_per_seq = seq // tm
    return pl.pallas_call(
        _mix_out_kernel,
        out_shape=jax.ShapeDtypeStruct((n, d), F32),
        grid=(n // tm,),
        in_specs=[pl.BlockSpec((tm, d), lambda i: (i, 0)),
                  pl.BlockSpec((d, d), lambda i: (0, 0), pipeline_mode=pl.Buffered(1)),
                  pl.BlockSpec((tm, d), lambda i: (i, 0)),
                  pl.BlockSpec((1, 6, d), lambda i: (i // tiles_per_seq, 0, 0)),
                  pl.BlockSpec((1, d), lambda i: (0, 0))],
        out_specs=pl.BlockSpec((tm, d), lambda i: (i, 0)),
        compiler_params=_params(("arbitrary",), 48),
        name="mix_out",
    )(mixed, w_out, x2, mod_l, g.reshape(1, d))


def _swiglu_accumulate(h, w1_ref, w3_ref, w2_ref, o_ref):
    a1 = jnp.dot(h, w1_ref[...], preferred_element_type=F32)
    a3 = jnp.dot(h, w3_ref[...], preferred_element_type=F32)
    act = (a1 * jax.nn.sigmoid(a1) * a3).astype(BF16)
    o_ref[...] += jnp.dot(act, w2_ref[...], preferred_element_type=F32)


def _dense_ffn_kernel(*refs, nf, n_cast):
    x_ref, mod_ref, gpre_ref, gpost_ref, w1_ref, w3_ref, w2_ref = refs[:7]
    cast_in = refs[7:7 + n_cast]
    o_ref = refs[7 + n_cast]
    cast_out = refs[8 + n_cast:8 + 2 * n_cast]
    h_scr = refs[8 + 2 * n_cast]
    f = pl.program_id(1)

    @pl.when(f == 0)
    def _():
        h = _modulated_rms(x_ref[...], gpre_ref[...], mod_ref[0, 4:5, :], mod_ref[0, 3:4, :])
        h_scr[...] = h.astype(BF16)
        o_ref[...] = jnp.zeros_like(o_ref)

    _swiglu_accumulate(h_scr[...], w1_ref, w3_ref, w2_ref, o_ref)
    for src, dst in zip(cast_in, cast_out):
        dst[...] = src[...].astype(BF16)

    @pl.when(f == nf - 1)
    def _():
        o_ref[...] = x_ref[...] + mod_ref[0, 5:6, :] * _rms(o_ref[...], gpost_ref[...])


def _dense_ffn(x2, mod_l, g_pre, g_post, w1, w3, w2, cast_weights, *, seq):
    n, d = x2.shape
    dff = w1.shape[1]
    tm, tf = DENSE_FFN_ROWS, FFN_COLS
    nf = dff // tf
    n_i = n // tm
    tiles_per_seq = seq // tm
    assert nf >= CAST_SPLIT or not cast_weights
    views, cast_specs = [], []
    for w in cast_weights:
        v = w.reshape(-1, w.shape[-1])
        rows, cols = v.shape
        assert rows % (n_i * 16) == 0 and cols % (CAST_SPLIT * LANE) == 0, v.shape
        views.append(v)
        cast_specs.append(pl.BlockSpec((rows // n_i, cols // CAST_SPLIT),
                                       lambda i, f: (i, jnp.minimum(f, CAST_SPLIT - 1))))
    outs = pl.pallas_call(
        functools.partial(_dense_ffn_kernel, nf=nf, n_cast=len(views)),
        out_shape=[jax.ShapeDtypeStruct((n, d), F32)]
                  + [jax.ShapeDtypeStruct(v.shape, BF16) for v in views],
        grid=(n_i, nf),
        in_specs=[pl.BlockSpec((tm, d), lambda i, f: (i, 0)),
                  pl.BlockSpec((1, 6, d), lambda i, f: (i // tiles_per_seq, 0, 0)),
                  pl.BlockSpec((1, d), lambda i, f: (0, 0)),
                  pl.BlockSpec((1, d), lambda i, f: (0, 0)),
                  pl.BlockSpec((d, tf), lambda i, f: (0, f)),
                  pl.BlockSpec((d, tf), lambda i, f: (0, f)),
                  pl.BlockSpec((tf, d), lambda i, f: (f, 0))] + cast_specs,
        out_specs=[pl.BlockSpec((tm, d), lambda i, f: (i, 0))] + cast_specs,
        scratch_shapes=[pltpu.VMEM((tm, d), BF16)],
        compiler_params=_params(("arbitrary", "arbitrary"), 56),
        name="dense_swiglu",
    )(x2, mod_l, g_pre.reshape(1, d), g_post.reshape(1, d), w1, w3, w2, *views)
    return outs[0], [o.reshape(w.shape) for o, w in zip(outs[1:], cast_weights)]


def _expert_ffn_kernel(te_ref, nv_ref, hs_ref, w1_ref, w3_ref, w2_ref, o_ref):
    i = pl.program_id(0)
    f = pl.program_id(1)

    @pl.when(f == 0)
    def _():
        o_ref[...] = jnp.zeros_like(o_ref)

    @pl.when(i < nv_ref[0])
    def _():
        _swiglu_accumulate(hs_ref[...].astype(BF16), w1_ref, w3_ref, w2_ref, o_ref)


def _expert_ffn(tile_expert, n_valid, hs, w1, w3, w2, *, tm):
    p, d = hs.shape
    dff = w1.shape[2]
    tf = FFN_COLS
    nf = dff // tf
    n_tiles = p // tm

    def row_map(i, f, te, nv):
        return (i, 0)

    def f_of(i, f, nv):
        return jnp.where(i < nv[0], f, nf - 1)

    grid_spec = pltpu.PrefetchScalarGridSpec(
        num_scalar_prefetch=2,
        grid=(n_tiles, nf),
        in_specs=[pl.BlockSpec((tm, d), row_map),
                  pl.BlockSpec((None, d, tf), lambda i, f, te, nv: (te[i], 0, f_of(i, f, nv))),
                  pl.BlockSpec((None, d, tf), lambda i, f, te, nv: (te[i], 0, f_of(i, f, nv))),
                  pl.BlockSpec((None, tf, d), lambda i, f, te, nv: (te[i], f_of(i, f, nv), 0))],
        out_specs=pl.BlockSpec((tm, d), row_map),
    )
    return pl.pallas_call(
        _expert_ffn_kernel,
        out_shape=jax.ShapeDtypeStruct((p, d), F32),
        grid_spec=grid_spec,
        compiler_params=_params(("arbitrary", "arbitrary"), 56),
        name="expert_swiglu",
    )(tile_expert, n_valid, hs, w1, w3, w2)


def _router_kernel(x_ref, mod_ref, g_ref, rw_ref, rb_ref, h_ref, route_ref, cnt_ref, carry, *, tm):
    i = pl.program_id(0)

    @pl.when(i == 0)
    def _():
        carry[...] = jnp.zeros_like(carry)

    h = _modulated_rms(x_ref[...], g_ref[...], mod_ref[0, 4:5, :], mod_ref[0, 3:4, :])
    h_ref[...] = h
    logits = jnp.dot(h.astype(BF16), rw_ref[...], preferred_element_type=F32) + rb_ref[...]
    lane = lax.broadcasted_iota(jnp.int32, (tm, LANE), 1)
    m1 = jnp.max(logits, axis=-1, keepdims=True)
    i1 = jnp.min(jnp.where(logits == m1, lane, LANE), axis=-1, keepdims=True)
    rest = jnp.where(lane == i1, -jnp.inf, logits)
    m2 = jnp.max(rest, axis=-1, keepdims=True)
    i2 = jnp.min(jnp.where(rest == m2, lane, LANE), axis=-1, keepdims=True)
    e = jnp.exp(m2 - m1)
    p1 = 1.0 / (1.0 + e)
    p2 = e / (1.0 + e)

    sel1 = lane == i1
    sel2 = lane == i2
    sel = jnp.logical_or(sel1, sel2).astype(F32)
    rows = lax.broadcasted_iota(jnp.int32, (tm, tm), 0)
    cols = lax.broadcasted_iota(jnp.int32, (tm, tm), 1)
    lower = (rows > cols).astype(BF16)
    rank = jnp.dot(lower, sel.astype(BF16), preferred_element_type=F32) + carry[...]
    r1 = jnp.sum(jnp.where(sel1, rank, 0.0), axis=-1, keepdims=True)
    r2 = jnp.sum(jnp.where(sel2, rank, 0.0), axis=-1, keepdims=True)
    carry[...] += jnp.sum(sel, axis=0, keepdims=True)
    cnt_ref[...] = carry[...]

    slab = jnp.where(lane == 0, i1.astype(F32), 0.0)
    slab = jnp.where(lane == 1, i2.astype(F32), slab)
    slab = jnp.where(lane == 2, r1, slab)
    slab = jnp.where(lane == 3, r2, slab)
    slab = jnp.where(lane == 4, p1, slab)
    slab = jnp.where(lane == 5, p2, slab)
    route_ref[...] = slab


def _router(x2, mod_l, g, rw_pad, rb_pad, *, seq):
    n, d = x2.shape
    tm = 512
    tiles_per_seq = seq // tm
    return pl.pallas_call(
        functools.partial(_router_kernel, tm=tm),
        out_shape=(jax.ShapeDtypeStruct((n, d), F32),
                   jax.ShapeDtypeStruct((n, LANE), F32),
                   jax.ShapeDtypeStruct((1, LANE), F32)),
        grid=(n // tm,),
        in_specs=[pl.BlockSpec((tm, d), lambda i: (i, 0)),
                  pl.BlockSpec((1, 6, d), lambda i: (i // tiles_per_seq, 0, 0)),
                  pl.BlockSpec((1, d), lambda i: (0, 0)),
                  pl.BlockSpec((d, LANE), lambda i: (0, 0)),
                  pl.BlockSpec((1, LANE), lambda i: (0, 0))],
        out_specs=(pl.BlockSpec((tm, d), lambda i: (i, 0)),
                   pl.BlockSpec((tm, LANE), lambda i: (i, 0)),
                   pl.BlockSpec((1, LANE), lambda i: (0, 0))),
        scratch_shapes=[pltpu.VMEM((1, LANE), F32)],
        compiler_params=_params(("arbitrary",), 40),
        name="router",
    )(x2, mod_l, g.reshape(1, d), rw_pad, rb_pad)


def _dispatch_kernel(pad_start_ref, pad_cnt_ref, pos_ref, h_ref, hs_ref, zero_scr, sem, zsem, *, tm):
    i = pl.program_id(0)

    def row_copy(r, k):
        return pltpu.make_async_copy(h_ref.at[pl.ds(r, 1), :],
                                     hs_ref.at[pl.ds(pos_ref[TOP_K * r + k], 1), :], sem)

    def issue(r, carry):
        for k in range(TOP_K):
            row_copy(r, k).start(priority=k)
        return carry
    lax.fori_loop(0, tm, issue, 0, unroll=ROW_DMA_UNROLL)

    @pl.when(i == 0)
    def _():
        zero_scr[...] = jnp.zeros_like(zero_scr)
        for e in range(N_EXPERTS + 1):
            def zcopy(j, e=e):
                return pltpu.make_async_copy(zero_scr.at[pl.ds(0, 1), :],
                                             hs_ref.at[pl.ds(pad_start_ref[e] + j, 1), :], zsem)

            def zissue(j, carry):
                zcopy(j).start()
                return carry

            def zwait(j, carry):
                zcopy(j).wait()
                return carry
            lax.fori_loop(0, pad_cnt_ref[e], zissue, 0)
            lax.fori_loop(0, pad_cnt_ref[e], zwait, 0)

    for k in range(TOP_K):
        pltpu.make_async_copy(h_ref, hs_ref.at[pl.ds(0, tm), :], sem).wait()


def _dispatch(pad_start, pad_cnt, pos_flat, h2, *, p_rows):
    n, d = h2.shape
    tm = 512
    grid_spec = pltpu.PrefetchScalarGridSpec(
        num_scalar_prefetch=2,
        grid=(n // tm,),
        in_specs=[pl.BlockSpec((TOP_K * tm,), lambda i, ps, pc: (i,), memory_space=pltpu.SMEM),
                  pl.BlockSpec((tm, d), lambda i, ps, pc: (i, 0))],
        out_specs=pl.BlockSpec(memory_space=pl.ANY),
        scratch_shapes=[pltpu.VMEM((8, d), F32),
                        pltpu.SemaphoreType.DMA(()),
                        pltpu.SemaphoreType.DMA(())],
    )
    return pl.pallas_call(
        functools.partial(_dispatch_kernel, tm=tm),
        out_shape=jax.ShapeDtypeStruct((p_rows, d), F32),
        grid_spec=grid_spec,
        compiler_params=_params(("arbitrary",), 32),
        name="dispatch_rows",
    )(pad_start, pad_cnt, pos_flat, h2)


def _combine_kernel(pos_ref, os_ref, x_ref, route_ref, mod_ref, g_ref, o_ref, a_scr, b_scr, sem, *, tm):
    def row_copy(r, k):
        dst = a_scr if k == 0 else b_scr
        return pltpu.make_async_copy(os_ref.at[pl.ds(pos_ref[TOP_K * r + k], 1), :],
                                     dst.at[pl.ds(r, 1), :], sem)

    def issue(r, carry):
        for k in range(TOP_K):
            row_copy(r, k).start(priority=k)
        return carry
    lax.fori_loop(0, tm, issue, 0, unroll=ROW_DMA_UNROLL)

    for dst in (a_scr, b_scr):
        pltpu.make_async_copy(os_ref.at[pl.ds(0, tm), :], dst, sem).wait()

    y = route_ref[:, 4:5] * a_scr[...] + route_ref[:, 5:6] * b_scr[...]
    o_ref[...] = x_ref[...] + mod_ref[0, 5:6, :] * _rms(y, g_ref[...])


def _combine(pos_flat, o_sorted, x2, route, mod_l, g, *, seq):
    n, d = x2.shape
    tm = 512
    tiles_per_seq = seq // tm
    return pl.pallas_call(
        functools.partial(_combine_kernel, tm=tm),
        out_shape=jax.ShapeDtypeStruct((n, d), F32),
        grid=(n // tm,),
        in_specs=[pl.BlockSpec((TOP_K * tm,), lambda i: (i,), memory_space=pltpu.SMEM),
                  pl.BlockSpec(memory_space=pl.ANY),
                  pl.BlockSpec((tm, d), lambda i: (i, 0)),
                  pl.BlockSpec((tm, LANE), lambda i: (i, 0)),
                  pl.BlockSpec((1, 6, d), lambda i: (i // tiles_per_seq, 0, 0)),
                  pl.BlockSpec((1, d), lambda i: (0, 0))],
        out_specs=pl.BlockSpec((tm, d), lambda i: (i, 0)),
        scratch_shapes=[pltpu.VMEM((tm, d), F32),
                        pltpu.VMEM((tm, d), F32),
                        pltpu.SemaphoreType.DMA(())],
        compiler_params=_params(("arbitrary",), 40),
        name="combine_rows",
    )(pos_flat, o_sorted, x2, route, mod_l, g.reshape(1, d))


def _moe_ffn(x2, mod_l, g_pre, g_post, router_w, router_b, w1, w3, w2, *, seq):
    n, d = x2.shape
    tm = EXPERT_ROWS
    n_tiles = (TOP_K * n) // tm + N_EXPERTS
    rw_pad = jnp.pad(router_w, ((0, 0), (0, LANE - N_EXPERTS))).astype(BF16)
    rb_pad = jnp.pad(router_b.reshape(1, -1), ((0, 0), (0, LANE - N_EXPERTS)),
                     constant_values=NEG_BIG)
    h2, route, cnt = _router(x2, mod_l, g_pre, rw_pad, rb_pad, seq=seq)

    counts = cnt[0, :N_EXPERTS].astype(jnp.int32)
    tiles_e = (counts + tm - 1) // tm
    tile_end = jnp.cumsum(tiles_e)
    group_start = (tile_end - tiles_e) * tm
    n_valid = tile_end[-1:]
    tile_ids = jnp.minimum(jnp.arange(n_tiles, dtype=jnp.int32), n_valid[0] - 1)
    tile_expert = jnp.sum(tile_ids[:, None] >= tile_end[None, :], axis=1).astype(jnp.int32)
    e_idx = route[:, 0:TOP_K].astype(jnp.int32)
    rank = route[:, TOP_K:2 * TOP_K].astype(jnp.int32)
    pos_flat = (group_start[e_idx] + rank).reshape(-1)

    pad_start = jnp.concatenate([group_start + counts, n_valid * tm])
    pad_cnt = jnp.concatenate([tiles_e * tm - counts, (n_tiles - n_valid) * tm])
    hs = _dispatch(pad_start, pad_cnt, pos_flat, h2, p_rows=n_tiles * tm)
    o_sorted = _expert_ffn(tile_expert, n_valid, hs, w1, w3, w2, tm=tm)
    return _combine(pos_flat, o_sorted, x2, route, mod_l, g_post, seq=seq)


def kernel(x, c, mod_w, mod_b, pre_mix_g, post_mix_g, pre_ffn_g, post_ffn_g, w_in, w_out, pool_w,
           pool_scale, sc_conv, cf_conv, cf_ln_g, cf_ln_b, ffn_w1, ffn_w3, ffn_w2, router_w,
           router_b, moe_w1, moe_w3, moe_w2):
    b, s, d = x.shape
    depth = mod_w.shape[0]
    mod = _modulation(c, mod_w, mod_b)
    x2 = x.reshape(b * s, d)
    moe_bf16 = {}
    for l in range(depth):
        mixed = _mix_in(x2, mod[l], pre_mix_g[l], w_in[l].astype(BF16), pool_w[l].astype(BF16),
                        pool_scale[l], sc_conv[l], cf_conv[l], cf_ln_g[l], cf_ln_b[l], seq=s)
        x2 = _mix_out(mixed, w_out[l].astype(BF16), x2, mod[l], post_mix_g[l], seq=s)
        j = l // 2
        if l % 2 == 0:
            nxt = [moe_w1[j], moe_w3[j], moe_w2[j]] if l + 1 < depth else []
            x2, cast = _dense_ffn(x2, mod[l], pre_ffn_g[l], post_ffn_g[l], ffn_w1[j].astype(BF16),
                                  ffn_w3[j].astype(BF16), ffn_w2[j].astype(BF16), nxt, seq=s)
            if cast:
                moe_bf16[j] = cast
        else:
            w1b, w3b, w2b = moe_bf16.get(j) or [w[j].astype(BF16) for w in (moe_w1, moe_w3, moe_w2)]
            x2 = _moe_ffn(x2, mod[l], pre_ffn_g[l], post_ffn_g[l], router_w[j], router_b[j],
                          w1b, w3b, w2b, seq=s)
    return x2.reshape(b, s, d)
```

```python
import functools

import jax
import jax.numpy as jnp
from jax import lax
from jax.experimental import pallas as pl
from jax.experimental.pallas import tpu as pltpu

F32 = jnp.float32
BF16 = jnp.bfloat16

EPS = 1e-6
LANE = 128
POOL_WINDOWS = (2, 4, 8, 16)
POOL_GROUP = 128
POOL_WIDTH = POOL_GROUP * len(POOL_WINDOWS)
SC_KERNEL = 3
CF_KERNEL = 31
N_EXPERTS = 8
TOP_K = 2

POOL_HIST = 16
SC_HIST = 8
CF_HIST = 32

NEG_BIG = -1e30

DENSE_FFN_ROWS = 512
EXPERT_ROWS = 512
EXPERT_COLS = 1024
FFN_COLS = 512
MIX_ROWS = 256
MIX_ROW_CHUNK = 64
PROJ_PIECE = 256
MIX_LANES_PER_TRIP = 3
ROW_DMA_UNROLL = 8
CAST_SPLIT = 8

MIB = 1024 * 1024


def _params(sem, vmem_mib):
    return pltpu.CompilerParams(dimension_semantics=sem, vmem_limit_bytes=vmem_mib * MIB)


def _modulated_rms(x, g, scale, shift):
    ms = jnp.mean(x * x, axis=-1, keepdims=True)
    return (x * lax.rsqrt(ms + EPS) * g) * (1.0 + scale) + shift


def _rms(y, g):
    ms = jnp.mean(y * y, axis=-1, keepdims=True)
    return y * lax.rsqrt(ms + EPS) * g


def _mod_kernel(c_ref, w_ref, b_ref, o_ref):
    c = c_ref[...]
    ca = (c * jax.nn.sigmoid(c)).astype(BF16)
    o_ref[0] = jnp.dot(ca, w_ref[0].astype(BF16), preferred_element_type=F32) + b_ref[0]


def _modulation(c, mod_w, mod_b):
    depth, d, n6 = mod_w.shape
    b = c.shape[0]
    tn = 1024
    c8 = jnp.pad(c, ((0, 8 - b), (0, 0)))
    out = pl.pallas_call(
        _mod_kernel,
        out_shape=jax.ShapeDtypeStruct((depth, 8, n6), F32),
        grid=(depth, n6 // tn),
        in_specs=[pl.BlockSpec((8, d), lambda l, j: (0, 0)),
                  pl.BlockSpec((1, d, tn), lambda l, j: (l, 0, j)),
                  pl.BlockSpec((1, 1, tn), lambda l, j: (l, 0, j))],
        out_specs=pl.BlockSpec((1, 8, tn), lambda l, j: (l, 0, j)),
        compiler_params=_params(("arbitrary", "arbitrary"), 40),
        name="adaln_mod",
    )(c8, mod_w, mod_b.reshape(depth, 1, n6))
    return out[:, :b].reshape(depth, b, 6, d)


def _mix_in_kernel(x_ref, mod_ref, g_ref, win_ref, poolw_ref, pscale_ref, scw_ref, cfw_ref,
                   lng_ref, lnb_ref, o_ref, p_ref, hb_ref, vbuf, sbuf, ubuf, bbuf, cbuf, *, tiles_per_seq,
                   n_tiles, tm, rc, sc_width, cf_width, static_trips):
    i = pl.program_id(0)
    o_sc = POOL_WIDTH
    o_c = o_sc + sc_width
    o_h = o_c + sc_width
    o_a = o_h + sc_width
    o_g = o_a + cf_width

    @pl.when(i == 0)
    def _():
        vbuf[...] = jnp.zeros_like(vbuf)
        sbuf[...] = jnp.zeros_like(sbuf)
        ubuf[...] = jnp.zeros_like(ubuf)
        bbuf[...] = jnp.zeros_like(bbuf)

    staged_in_seq = (jnp.maximum(i, 1) - 1) % tiles_per_seq

    def window(buf, hist, r0, back, n, lanes):
        return buf[pl.ds(r0 + (hist - back), n), lanes]

    def pool_rows(r0):
        t1 = (staged_in_seq * tm + r0 + 1 + lax.broadcasted_iota(jnp.int32, (rc, 1), 0)).astype(F32)
        for g, w in enumerate(POOL_WINDOWS):
            lanes = slice(g * POOL_GROUP, (g + 1) * POOL_GROUP)
            win = window(vbuf, POOL_HIST, r0, POOL_HIST, rc + POOL_HIST, lanes)
            cur = win[POOL_HIST:, :]
            acc = cur
            for j in range(1, w):
                acc = acc + win[POOL_HIST - j:POOL_HIST - j + rc, :]
            u = acc * (1.0 / jnp.minimum(t1, float(w))) - cur
            z = jnp.dot(u.astype(BF16), poolw_ref[g], preferred_element_type=F32)
            o_ref[pl.ds(r0, rc), lanes] = (z * pscale_ref[:, lanes]).astype(o_ref.dtype)

    def sc_chunk(c0, r0):
        lanes = pl.ds(c0, LANE)
        win = window(sbuf, SC_HIST, r0, SC_HIST, rc + SC_HIST, lanes)
        acc = None
        for k in range(SC_KERNEL):
            s0 = SC_HIST - (SC_KERNEL - 1) + k
            term = scw_ref[k:k + 1, lanes] * win[s0:s0 + rc, :]
            acc = term if acc is None else acc + term
        y = bbuf[pl.ds(r0, rc), lanes] * acc
        o_ref[pl.ds(r0, rc), pl.ds(POOL_WIDTH + c0, LANE)] = y.astype(o_ref.dtype)

    def cf_chunk(c0, r0):
        lanes = pl.ds(c0, LANE)
        acc = None
        for r in range(8):
            back = 8 if r else 0
            part = None
            for q in range((CF_KERNEL - 1 - r) // 8 + 1):
                k = CF_KERNEL - 1 - 8 * q - r
                term = cfw_ref[k:k + 1, lanes] * window(ubuf, CF_HIST, r0, back + 8 * q,
                                                        rc + back, lanes)
                part = term if part is None else part + term
            if r:
                part = part[8 - r:8 - r + rc, :]
            acc = part if acc is None else acc + part
        cbuf[pl.ds(r0, rc), lanes] = acc

    sc_in_trips = sc_width == cf_width
    for r0 in range(0, tm, rc):
        pool_rows(r0)
        if not sc_in_trips:
            for c0 in range(0, sc_width, LANE):
                sc_chunk(c0, r0)

    h = _modulated_rms(x_ref[...], g_ref[...], mod_ref[0, 1:2, :], mod_ref[0, 0:1, :])
    hb_ref[...] = h.astype(BF16)

    def project(pc):
        cols = pl.ds(pc, PROJ_PIECE)
        p_ref[:, cols] = jnp.dot(hb_ref[...], win_ref[:, cols], preferred_element_type=F32)

    n_pieces = p_ref.shape[1] // PROJ_PIECE
    n_trips = cf_width // (MIX_LANES_PER_TRIP * LANE)
    per_trip = n_pieces // n_trips
    for j in range(per_trip * n_trips, n_pieces):
        project(j * PROJ_PIECE)

    def aligned(v, m):
        return v if isinstance(v, int) else pl.multiple_of(v, m)

    def trip(it):
        for m in range(per_trip):
            project(aligned((it * per_trip + m) * PROJ_PIECE, PROJ_PIECE))
        for m in range(MIX_LANES_PER_TRIP):
            c0 = aligned((it * MIX_LANES_PER_TRIP + m) * LANE, LANE)
            for r0 in range(0, tm, rc):
                cf_chunk(c0, r0)
                if sc_in_trips:
                    sc_chunk(c0, r0)

    if static_trips:
        for it in range(n_trips):
            pl.when(i >= 0)(functools.partial(trip, it))
    else:
        def body(it, carry):
            trip(it)
            return carry
        lax.fori_loop(0, n_trips, body, 0)

    u = cbuf[...]
    mu = jnp.mean(u, axis=-1, keepdims=True)
    d = u - mu
    var = jnp.mean(d * d, axis=-1, keepdims=True)
    ln = d * lax.rsqrt(var + EPS) * lng_ref[...] + lnb_ref[...]
    o_ref[:, POOL_WIDTH + sc_width:] = (ln * jax.nn.sigmoid(ln)).astype(o_ref.dtype)

    keep = (jnp.minimum(i, n_tiles - 1) % tiles_per_seq) != 0
    vbuf[0:POOL_HIST, :] = jnp.where(keep, vbuf[tm:tm + POOL_HIST, :], 0.0)
    sbuf[0:SC_HIST, :] = jnp.where(keep, sbuf[tm:tm + SC_HIST, :], 0.0)
    ubuf[0:CF_HIST, :] = jnp.where(keep, ubuf[tm:tm + CF_HIST, :], 0.0)
    vbuf[POOL_HIST:POOL_HIST + tm, :] = p_ref[:, 0:POOL_WIDTH]
    bbuf[...] = p_ref[:, o_sc:o_sc + sc_width]
    sbuf[SC_HIST:SC_HIST + tm, :] = p_ref[:, o_c:o_c + sc_width] * p_ref[:, o_h:o_h + sc_width]
    ubuf[CF_HIST:CF_HIST + tm, :] = (p_ref[:, o_a:o_a + cf_width]
                                     * jax.nn.sigmoid(p_ref[:, o_g:o_g + cf_width]))


def _mix_in(x2, mod_l, g, w_in, pool_w, pool_scale, sc_conv, cf_conv, ln_g, ln_b, *, seq, static_trips):
    n, d = x2.shape
    d_in = w_in.shape[1]
    sc_width = sc_conv.shape[1]
    cf_width = cf_conv.shape[1]
    tm = MIX_ROWS
    n_tiles = n // tm
    tiles_per_seq = seq // tm
    const = lambda i: (0, 0)
    cur = lambda i: jnp.minimum(i, n_tiles - 1)
    assert d_in % PROJ_PIECE == 0 and cf_width % (MIX_LANES_PER_TRIP * LANE) == 0
    return pl.pallas_call(
        functools.partial(_mix_in_kernel, tiles_per_seq=tiles_per_seq, n_tiles=n_tiles, tm=tm,
                          rc=MIX_ROW_CHUNK, sc_width=sc_width, cf_width=cf_width,
                          static_trips=static_trips),
        out_shape=jax.ShapeDtypeStruct((n, d), BF16),
        grid=(n_tiles + 1,),
        in_specs=[pl.BlockSpec((tm, d), lambda i: (cur(i), 0)),
                  pl.BlockSpec((1, 6, d), lambda i: (cur(i) // tiles_per_seq, 0, 0)),
                  pl.BlockSpec((1, d), const),
                  pl.BlockSpec((d, d_in), const, pipeline_mode=pl.Buffered(1)),
                  pl.BlockSpec(pool_w.shape, lambda i: (0, 0, 0)),
                  pl.BlockSpec((1, POOL_WIDTH), const),
                  pl.BlockSpec(sc_conv.shape, const),
                  pl.BlockSpec(cf_conv.shape, const),
                  pl.BlockSpec((1, cf_width), const),
                  pl.BlockSpec((1, cf_width), const)],
        out_specs=pl.BlockSpec((tm, d), lambda i: (jnp.maximum(i, 1) - 1, 0)),
        scratch_shapes=[pltpu.VMEM((tm, d_in), F32),
                        pltpu.VMEM((tm, d), BF16),
                        pltpu.VMEM((POOL_HIST + tm, POOL_WIDTH), F32),
                        pltpu.VMEM((SC_HIST + tm, sc_width), F32),
                        pltpu.VMEM((CF_HIST + tm, cf_width), F32),
                        pltpu.VMEM((tm, sc_width), F32),
                        pltpu.VMEM((tm, cf_width), F32)],
        compiler_params=_params(("arbitrary",), 48),
        name="mix_in",
    )(x2, mod_l, g.reshape(1, d), w_in, pool_w, pool_scale.reshape(1, -1), sc_conv, cf_conv,
      ln_g.reshape(1, -1), ln_b.reshape(1, -1))


def _mix_out_kernel(m_ref, w_ref, x_ref, mod_ref, g_ref, o_ref):
    y = jnp.dot(m_ref[...], w_ref[...], preferred_element_type=F32)
    o_ref[...] = x_ref[...] + mod_ref[0, 2:3, :] * _rms(y, g_ref[...])


def _mix_out(mixed, w_out, x2, mod_l, g, *, seq):
    n, d = x2.shape
    tm = 512
    tiles_per_seq = seq // tm
    return pl.pallas_call(
        _mix_out_kernel,
        out_shape=jax.ShapeDtypeStruct((n, d), F32),
        grid=(n // tm,),
        in_specs=[pl.BlockSpec((tm, d), lambda i: (i, 0)),
                  pl.BlockSpec((d, d), lambda i: (0, 0), pipeline_mode=pl.Buffered(1)),
                  pl.BlockSpec((tm, d), lambda i: (i, 0)),
                  pl.BlockSpec((1, 6, d), lambda i: (i // tiles_per_seq, 0, 0)),
                  pl.BlockSpec((1, d), lambda i: (0, 0))],
        out_specs=pl.BlockSpec((tm, d), lambda i: (i, 0)),
        compiler_params=_params(("arbitrary",), 48),
        name="mix_out",
    )(mixed, w_out, x2, mod_l, g.reshape(1, d))


def _swiglu_accumulate(h, w1_ref, w3_ref, w2_ref, o_ref):
    a1 = jnp.dot(h, w1_ref[...], preferred_element_type=F32)
    a3 = jnp.dot(h, w3_ref[...], preferred_element_type=F32)
    act = (a1 * jax.nn.sigmoid(a1) * a3).astype(BF16)
    o_ref[...] += jnp.dot(act, w2_ref[...], preferred_element_type=F32)


def _dense_ffn_kernel(*refs, nf, n_cast):
    x_ref, mod_ref, gpre_ref, gpost_ref, w1_ref, w3_ref, w2_ref = refs[:7]
    cast_in = refs[7:7 + n_cast]
    o_ref = refs[7 + n_cast]
    cast_out = refs[8 + n_cast:8 + 2 * n_cast]
    h_scr = refs[8 + 2 * n_cast]
    f = pl.program_id(1)

    @pl.when(f == 0)
    def _():
        h = _modulated_rms(x_ref[...], gpre_ref[...], mod_ref[0, 4:5, :], mod_ref[0, 3:4, :])
        h_scr[...] = h.astype(BF16)
        o_ref[...] = jnp.zeros_like(o_ref)

    _swiglu_accumulate(h_scr[...], w1_ref, w3_ref, w2_ref, o_ref)
    for src, dst in zip(cast_in, cast_out):
        dst[...] = src[...].astype(BF16)

    @pl.when(f == nf - 1)
    def _():
        o_ref[...] = x_ref[...] + mod_ref[0, 5:6, :] * _rms(o_ref[...], gpost_ref[...])


def _dense_ffn(x2, mod_l, g_pre, g_post, w1, w3, w2, cast_weights, *, seq):
    n, d = x2.shape
    dff = w1.shape[1]
    tm, tf = DENSE_FFN_ROWS, FFN_COLS
    nf = dff // tf
    n_i = n // tm
    tiles_per_seq = seq // tm
    assert nf >= CAST_SPLIT or not cast_weights
    views, cast_specs = [], []
    for w in cast_weights:
        v = w.reshape(-1, w.shape[-1])
        rows, cols = v.shape
        assert rows % (n_i * 16) == 0 and cols % (CAST_SPLIT * LANE) == 0, v.shape
        views.append(v)
        cast_specs.append(pl.BlockSpec((rows // n_i, cols // CAST_SPLIT),
                                       lambda i, f: (i, jnp.minimum(f, CAST_SPLIT - 1))))
    outs = pl.pallas_call(
        functools.partial(_dense_ffn_kernel, nf=nf, n_cast=len(views)),
        out_shape=[jax.ShapeDtypeStruct((n, d), F32)]
                  + [jax.ShapeDtypeStruct(v.shape, BF16) for v in views],
        grid=(n_i, nf),
        in_specs=[pl.BlockSpec((tm, d), lambda i, f: (i, 0)),
                  pl.BlockSpec((1, 6, d), lambda i, f: (i // tiles_per_seq, 0, 0)),
                  pl.BlockSpec((1, d), lambda i, f: (0, 0)),
                  pl.BlockSpec((1, d), lambda i, f: (0, 0)),
                  pl.BlockSpec((d, tf), lambda i, f: (0, f)),
                  pl.BlockSpec((d, tf), lambda i, f: (0, f)),
                  pl.BlockSpec((tf, d), lambda i, f: (f, 0))] + cast_specs,
        out_specs=[pl.BlockSpec((tm, d), lambda i, f: (i, 0))] + cast_specs,
        scratch_shapes=[pltpu.VMEM((tm, d), BF16)],
        compiler_params=_params(("arbitrary", "arbitrary"), 56),
        name="dense_swiglu",
    )(x2, mod_l, g_pre.reshape(1, d), g_post.reshape(1, d), w1, w3, w2, *views)
    return outs[0], [o.reshape(w.shape) for o, w in zip(outs[1:], cast_weights)]


def _expert_ffn_kernel(te_ref, nv_ref, hs_ref, w1_ref, w3_ref, w2_ref, o_ref):
    i = pl.program_id(0)
    f = pl.program_id(1)

    @pl.when(f == 0)
    def _():
        o_ref[...] = jnp.zeros_like(o_ref)

    @pl.when(i < nv_ref[0])
    def _():
        _swiglu_accumulate(hs_ref[...].astype(BF16), w1_ref, w3_ref, w2_ref, o_ref)


def _expert_ffn(tile_expert, n_valid, hs, w1, w3, w2, *, tm):
    p, d = hs.shape
    dff = w1.shape[2]
    tf = EXPERT_COLS
    nf = dff // tf
    n_tiles = p // tm

    def row_map(i, f, te, nv):
        return (i, 0)

    def f_of(i, f, nv):
        return jnp.where(i < nv[0], f, nf - 1)

    grid_spec = pltpu.PrefetchScalarGridSpec(
        num_scalar_prefetch=2,
        grid=(n_tiles, nf),
        in_specs=[pl.BlockSpec((tm, d), row_map),
                  pl.BlockSpec((None, d, tf), lambda i, f, te, nv: (te[i], 0, f_of(i, f, nv))),
                  pl.BlockSpec((None, d, tf), lambda i, f, te, nv: (te[i], 0, f_of(i, f, nv))),
                  pl.BlockSpec((None, tf, d), lambda i, f, te, nv: (te[i], f_of(i, f, nv), 0))],
        out_specs=pl.BlockSpec((tm, d), row_map),
    )
    return pl.pallas_call(
        _expert_ffn_kernel,
        out_shape=jax.ShapeDtypeStruct((p, d), F32),
        grid_spec=grid_spec,
        compiler_params=_params(("arbitrary", "arbitrary"), 56),
        name="expert_swiglu",
    )(tile_expert, n_valid, hs, w1, w3, w2)


def _router_kernel(x_ref, mod_ref, g_ref, rw_ref, rb_ref, h_ref, route_ref, cnt_ref, carry, *, tm):
    i = pl.program_id(0)

    @pl.when(i == 0)
    def _():
        carry[...] = jnp.zeros_like(carry)

    h = _modulated_rms(x_ref[...], g_ref[...], mod_ref[0, 4:5, :], mod_ref[0, 3:4, :])
    h_ref[...] = h
    logits = jnp.dot(h.astype(BF16), rw_ref[...], preferred_element_type=F32) + rb_ref[...]
    lane = lax.broadcasted_iota(jnp.int32, (tm, LANE), 1)
    m1 = jnp.max(logits, axis=-1, keepdims=True)
    i1 = jnp.min(jnp.where(logits == m1, lane, LANE), axis=-1, keepdims=True)
    rest = jnp.where(lane == i1, -jnp.inf, logits)
    m2 = jnp.max(rest, axis=-1, keepdims=True)
    i2 = jnp.min(jnp.where(rest == m2, lane, LANE), axis=-1, keepdims=True)
    e = jnp.exp(m2 - m1)
    p1 = 1.0 / (1.0 + e)
    p2 = e / (1.0 + e)

    sel1 = lane == i1
    sel2 = lane == i2
    sel = jnp.logical_or(sel1, sel2).astype(F32)
    rows = lax.broadcasted_iota(jnp.int32, (tm, tm), 0)
    cols = lax.broadcasted_iota(jnp.int32, (tm, tm), 1)
    lower = (rows > cols).astype(BF16)
    rank = jnp.dot(lower, sel.astype(BF16), preferred_element_type=F32) + carry[...]
    r1 = jnp.sum(jnp.where(sel1, rank, 0.0), axis=-1, keepdims=True)
    r2 = jnp.sum(jnp.where(sel2, rank, 0.0), axis=-1, keepdims=True)
    carry[...] += jnp.sum(sel, axis=0, keepdims=True)
    cnt_ref[...] = carry[...]

    slab = jnp.where(lane == 0, i1.astype(F32), 0.0)
    slab = jnp.where(lane == 1, i2.astype(F32), slab)
    slab = jnp.where(lane == 2, r1, slab)
    slab = jnp.where(lane == 3, r2, slab)
    slab = jnp.where(lane == 4, p1, slab)
    slab = jnp.where(lane == 5, p2, slab)
    route_ref[...] = slab


def _router(x2, mod_l, g, rw_pad, rb_pad, *, seq):
    n, d = x2.shape
    tm = 512
    tiles_per_seq = seq // tm
    return pl.pallas_call(
        functools.partial(_router_kernel, tm=tm),
        out_shape=(jax.ShapeDtypeStruct((n, d), F32),
                   jax.ShapeDtypeStruct((n, LANE), F32),
                   jax.ShapeDtypeStruct((1, LANE), F32)),
        grid=(n // tm,),
        in_specs=[pl.BlockSpec((tm, d), lambda i: (i, 0)),
                  pl.BlockSpec((1, 6, d), lambda i: (i // tiles_per_seq, 0, 0)),
                  pl.BlockSpec((1, d), lambda i: (0, 0)),
                  pl.BlockSpec((d, LANE), lambda i: (0, 0)),
                  pl.BlockSpec((1, LANE), lambda i: (0, 0))],
        out_specs=(pl.BlockSpec((tm, d), lambda i: (i, 0)),
                   pl.BlockSpec((tm, LANE), lambda i: (i, 0)),
                   pl.BlockSpec((1, LANE), lambda i: (0, 0))),
        scratch_shapes=[pltpu.VMEM((1, LANE), F32)],
        compiler_params=_params(("arbitrary",), 40),
        name="router",
    )(x2, mod_l, g.reshape(1, d), rw_pad, rb_pad)


def _dispatch_kernel(pad_start_ref, pad_cnt_ref, pos_ref, h_ref, hs_ref, zero_scr, sem, zsem, *, tm):
    i = pl.program_id(0)

    def row_copy(r, k):
        return pltpu.make_async_copy(h_ref.at[pl.ds(r, 1), :],
                                     hs_ref.at[pl.ds(pos_ref[TOP_K * r + k], 1), :], sem)

    def issue(r, carry):
        for k in range(TOP_K):
            row_copy(r, k).start(priority=k)
        return carry
    lax.fori_loop(0, tm, issue, 0, unroll=ROW_DMA_UNROLL)

    @pl.when(i == 0)
    def _():
        zero_scr[...] = jnp.zeros_like(zero_scr)
        for e in range(N_EXPERTS + 1):
            def zcopy(j, e=e):
                return pltpu.make_async_copy(zero_scr.at[pl.ds(0, 1), :],
                                             hs_ref.at[pl.ds(pad_start_ref[e] + j, 1), :], zsem)

            def zissue(j, carry):
                zcopy(j).start()
                return carry

            def zwait(j, carry):
                zcopy(j).wait()
                return carry
            lax.fori_loop(0, pad_cnt_ref[e], zissue, 0)
            lax.fori_loop(0, pad_cnt_ref[e], zwait, 0)

    for k in range(TOP_K):
        pltpu.make_async_copy(h_ref, hs_ref.at[pl.ds(0, tm), :], sem).wait()


def _dispatch(pad_start, pad_cnt, pos_flat, h2, *, p_rows):
    n, d = h2.shape
    tm = 512
    grid_spec = pltpu.PrefetchScalarGridSpec(
        num_scalar_prefetch=2,
        grid=(n // tm,),
        in_specs=[pl.BlockSpec((TOP_K * tm,), lambda i, ps, pc: (i,), memory_space=pltpu.SMEM),
                  pl.BlockSpec((tm, d), lambda i, ps, pc: (i, 0))],
        out_specs=pl.BlockSpec(memory_space=pl.ANY),
        scratch_shapes=[pltpu.VMEM((8, d), F32),
                        pltpu.SemaphoreType.DMA(()),
                        pltpu.SemaphoreType.DMA(())],
    )
    return pl.pallas_call(
        functools.partial(_dispatch_kernel, tm=tm),
        out_shape=jax.ShapeDtypeStruct((p_rows, d), F32),
        grid_spec=grid_spec,
        compiler_params=_params(("arbitrary",), 32),
        name="dispatch_rows",
    )(pad_start, pad_cnt, pos_flat, h2)


def _combine_kernel(pos_ref, os_ref, x_ref, route_ref, mod_ref, g_ref, o_ref, a_scr, b_scr, sem, *, tm):
    def row_copy(r, k):
        dst = a_scr if k == 0 else b_scr
        return pltpu.make_async_copy(os_ref.at[pl.ds(pos_ref[TOP_K * r + k], 1), :],
                                     dst.at[pl.ds(r, 1), :], sem)

    def issue(r, carry):
        for k in range(TOP_K):
            row_copy(r, k).start(priority=k)
        return carry
    lax.fori_loop(0, tm, issue, 0, unroll=ROW_DMA_UNROLL)

    for dst in (a_scr, b_scr):
        pltpu.make_async_copy(os_ref.at[pl.ds(0, tm), :], dst, sem).wait()

    y = route_ref[:, 4:5] * a_scr[...] + route_ref[:, 5:6] * b_scr[...]
    o_ref[...] = x_ref[...] + mod_ref[0, 5:6, :] * _rms(y, g_ref[...])


def _combine(pos_flat, o_sorted, x2, route, mod_l, g, *, seq):
    n, d = x2.shape
    tm = 512
    tiles_per_seq = seq // tm
    return pl.pallas_call(
        functools.partial(_combine_kernel, tm=tm),
        out_shape=jax.ShapeDtypeStruct((n, d), F32),
        grid=(n // tm,),
        in_specs=[pl.BlockSpec((TOP_K * tm,), lambda i: (i,), memory_space=pltpu.SMEM),
                  pl.BlockSpec(memory_space=pl.ANY),
                  pl.BlockSpec((tm, d), lambda i: (i, 0)),
                  pl.BlockSpec((tm, LANE), lambda i: (i, 0)),
                  pl.BlockSpec((1, 6, d), lambda i: (i // tiles_per_seq, 0, 0)),
                  pl.BlockSpec((1, d), lambda i: (0, 0))],
        out_specs=pl.BlockSpec((tm, d), lambda i: (i, 0)),
        scratch_shapes=[pltpu.VMEM((tm, d), F32),
                        pltpu.VMEM((tm, d), F32),
                        pltpu.SemaphoreType.DMA(())],
        compiler_params=_params(("arbitrary",), 40),
        name="combine_rows",
    )(pos_flat, o_sorted, x2, route, mod_l, g.reshape(1, d))


def _moe_ffn(x2, mod_l, g_pre, g_post, router_w, router_b, w1, w3, w2, *, seq):
    n, d = x2.shape
    tm = EXPERT_ROWS
    n_tiles = (TOP_K * n) // tm + N_EXPERTS
    rw_pad = jnp.pad(router_w, ((0, 0), (0, LANE - N_EXPERTS))).astype(BF16)
    rb_pad = jnp.pad(router_b.reshape(1, -1), ((0, 0), (0, LANE - N_EXPERTS)),
                     constant_values=NEG_BIG)
    h2, route, cnt = _router(x2, mod_l, g_pre, rw_pad, rb_pad, seq=seq)

    counts = cnt[0, :N_EXPERTS].astype(jnp.int32)
    tiles_e = (counts + tm - 1) // tm
    tile_end = jnp.cumsum(tiles_e)
    group_start = (tile_end - tiles_e) * tm
    n_valid = tile_end[-1:]
    tile_ids = jnp.minimum(jnp.arange(n_tiles, dtype=jnp.int32), n_valid[0] - 1)
    tile_expert = jnp.sum(tile_ids[:, None] >= tile_end[None, :], axis=1).astype(jnp.int32)
    e_idx = route[:, 0:TOP_K].astype(jnp.int32)
    rank = route[:, TOP_K:2 * TOP_K].astype(jnp.int32)
    pos_flat = (group_start[e_idx] + rank).reshape(-1)

    pad_start = jnp.concatenate([group_start + counts, n_valid * tm])
    pad_cnt = jnp.concatenate([tiles_e * tm - counts, (n_tiles - n_valid) * tm])
    hs = _dispatch(pad_start, pad_cnt, pos_flat, h2, p_rows=n_tiles * tm)
    o_sorted = _expert_ffn(tile_expert, n_valid, hs, w1, w3, w2, tm=tm)
    return _combine(pos_flat, o_sorted, x2, route, mod_l, g_post, seq=seq)


def kernel(x, c, mod_w, mod_b, pre_mix_g, post_mix_g, pre_ffn_g, post_ffn_g, w_in, w_out, pool_w,
           pool_scale, sc_conv, cf_conv, cf_ln_g, cf_ln_b, ffn_w1, ffn_w3, ffn_w2, router_w,
           router_b, moe_w1, moe_w3, moe_w2):
    b, s, d = x.shape
    depth = mod_w.shape[0]
    mod = _modulation(c, mod_w, mod_b)
    x2 = x.reshape(b * s, d)
    moe_bf16 = {}
    for l in range(depth):
        mixed = _mix_in(x2, mod[l], pre_mix_g[l], w_in[l].astype(BF16), pool_w[l].astype(BF16),
                        pool_scale[l], sc_conv[l], cf_conv[l], cf_ln_g[l], cf_ln_b[l], seq=s,
                        static_trips=(l % 2 == 1))
        x2 = _mix_out(mixed, w_out[l].astype(BF16), x2, mod[l], post_mix_g[l], seq=s)
        j = l // 2
        if l % 2 == 0:
            nxt = [moe_w1[j], moe_w3[j], moe_w2[j]] if l + 1 < depth else []
            x2, cast = _dense_ffn(x2, mod[l], pre_ffn_g[l], post_ffn_g[l], ffn_w1[j].astype(BF16),
                                  ffn_w3[j].astype(BF16), ffn_w2[j].astype(BF16), nxt, seq=s)
            if cast:
                moe_bf16[j] = cast
        else:
            w1b, w3b, w2b = moe_bf16.get(j) or [w[j].astype(BF16) for w in (moe_w1, moe_w3, moe_w2)]
            x2 = _moe_ffn(x2, mod[l], pre_ffn_g[l], post_ffn_g[l], router_w[j], router_b[j],
                          w1b, w3b, w2b, seq=s)
    return x2.reshape(b, s, d)
```

```python
import functools

import jax
import jax.numpy as jnp
from jax import lax
from jax.experimental import pallas as pl
from jax.experimental.pallas import tpu as pltpu

F32 = jnp.float32
BF16 = jnp.bfloat16

EPS = 1e-6
LANE = 128
POOL_WINDOWS = (2, 4, 8, 16)
POOL_GROUP = 128
POOL_WIDTH = POOL_GROUP * len(POOL_WINDOWS)
SC_KERNEL = 3
CF_KERNEL = 31
N_EXPERTS = 8
TOP_K = 2

POOL_HIST = 16
SC_HIST = 8
CF_HIST = 32

NEG_BIG = -1e30

DENSE_FFN_ROWS = 512
EXPERT_ROWS = 512
EXPERT_COLS = 1024
FFN_COLS = 512
MIX_ROWS = 256
MIX_ROW_CHUNK = 64
PROJ_PIECE = 256
ROW_DMA_UNROLL = 8
CAST_SPLIT = 8

MIB = 1024 * 1024


def _params(sem, vmem_mib):
    return pltpu.CompilerParams(dimension_semantics=sem, vmem_limit_bytes=vmem_mib * MIB)


def _modulated_rms(x, g, scale, shift):
    ms = jnp.mean(x * x, axis=-1, keepdims=True)
    return (x * lax.rsqrt(ms + EPS) * g) * (1.0 + scale) + shift


def _rms(y, g):
    ms = jnp.mean(y * y, axis=-1, keepdims=True)
    return y * lax.rsqrt(ms + EPS) * g


def _mod_kernel(c_ref, w_ref, b_ref, o_ref):
    c = c_ref[...]
    ca = (c * jax.nn.sigmoid(c)).astype(BF16)
    o_ref[0] = jnp.dot(ca, w_ref[0].astype(BF16), preferred_element_type=F32) + b_ref[0]


def _modulation(c, mod_w, mod_b):
    depth, d, n6 = mod_w.shape
    b = c.shape[0]
    tn = 1024
    c8 = jnp.pad(c, ((0, 8 - b), (0, 0)))
    out = pl.pallas_call(
        _mod_kernel,
        out_shape=jax.ShapeDtypeStruct((depth, 8, n6), F32),
        grid=(depth, n6 // tn),
        in_specs=[pl.BlockSpec((8, d), lambda l, j: (0, 0)),
                  pl.BlockSpec((1, d, tn), lambda l, j: (l, 0, j)),
                  pl.BlockSpec((1, 1, tn), lambda l, j: (l, 0, j))],
        out_specs=pl.BlockSpec((1, 8, tn), lambda l, j: (l, 0, j)),
        compiler_params=_params(("arbitrary", "arbitrary"), 40),
        name="adaln_mod",
    )(c8, mod_w, mod_b.reshape(depth, 1, n6))
    return out[:, :b].reshape(depth, b, 6, d)


def _mix_in_kernel(x_ref, mod_ref, g_ref, win_ref, poolw_ref, pscale_ref, scw_ref, cfw_ref,
                   lng_ref, lnb_ref, o_ref, p_ref, hb_ref, vbuf, sbuf, ubuf, bbuf, cbuf, *, tiles_per_seq,
                   n_tiles, tm, rc, sc_width, cf_width):
    i = pl.program_id(0)
    o_sc = POOL_WIDTH
    o_c = o_sc + sc_width
    o_h = o_c + sc_width
    o_a = o_h + sc_width
    o_g = o_a + cf_width

    @pl.when(i == 0)
    def _():
        vbuf[...] = jnp.zeros_like(vbuf)
        sbuf[...] = jnp.zeros_like(sbuf)
        ubuf[...] = jnp.zeros_like(ubuf)
        bbuf[...] = jnp.zeros_like(bbuf)

    staged_in_seq = (jnp.maximum(i, 1) - 1) % tiles_per_seq
    t1 = (staged_in_seq * tm + 1 + lax.broadcasted_iota(jnp.int32, (tm, 1), 0)).astype(F32)
    work = []

    def pool_item(g, w, r0, inv_den):
        lanes = slice(g * POOL_GROUP, (g + 1) * POOL_GROUP)
        cur = vbuf[POOL_HIST + r0:POOL_HIST + r0 + rc, lanes]
        acc = cur
        for j in range(1, w):
            acc = acc + vbuf[POOL_HIST + r0 - j:POOL_HIST + r0 - j + rc, lanes]
        u = acc * inv_den[r0:r0 + rc, :] - cur
        z = jnp.dot(u.astype(BF16), poolw_ref[g], preferred_element_type=F32)
        o_ref[r0:r0 + rc, lanes] = (z * pscale_ref[:, lanes]).astype(o_ref.dtype)

    for g, w in enumerate(POOL_WINDOWS):
        inv_den = 1.0 / jnp.minimum(t1, float(w))
        for r0 in range(0, tm, rc):
            work.append((4 * w, functools.partial(pool_item, g, w, r0, inv_den)))

    def sc_item(c0, r0):
        lanes = slice(c0, c0 + LANE)
        acc = None
        for k in range(SC_KERNEL):
            s0 = SC_HIST + r0 - (SC_KERNEL - 1) + k
            term = scw_ref[k:k + 1, lanes] * sbuf[s0:s0 + rc, lanes]
            acc = term if acc is None else acc + term
        y = bbuf[r0:r0 + rc, lanes] * acc
        o_ref[r0:r0 + rc, POOL_WIDTH + c0:POOL_WIDTH + c0 + LANE] = y.astype(o_ref.dtype)

    for c0 in range(0, sc_width, LANE):
        for r0 in range(0, tm, rc):
            work.append((4 * SC_KERNEL, functools.partial(sc_item, c0, r0)))

    def cf_item(c0, r0):
        lanes = slice(c0, c0 + LANE)
        base = CF_HIST + r0
        acc = None
        for r in range(8):
            lo = base - (8 if r else 0)
            rows = rc + (8 if r else 0)
            part = None
            for q in range((CF_KERNEL - 1 - r) // 8 + 1):
                k = CF_KERNEL - 1 - 8 * q - r
                term = cfw_ref[k:k + 1, lanes] * ubuf[lo - 8 * q:lo - 8 * q + rows, lanes]
                part = term if part is None else part + term
            if r:
                part = part[8 - r:8 - r + rc, :]
            acc = part if acc is None else acc + part
        cbuf[r0:r0 + rc, lanes] = acc

    for c0 in range(0, cf_width, LANE):
        for r0 in range(0, tm, rc):
            work.append((3 * CF_KERNEL, functools.partial(cf_item, c0, r0)))

    h = _modulated_rms(x_ref[...], g_ref[...], mod_ref[0, 1:2, :], mod_ref[0, 0:1, :])
    hb_ref[...] = h.astype(BF16)
    d_in = p_ref.shape[1]
    pieces = range(0, d_in, PROJ_PIECE)
    per_piece = sum(cost for cost, _ in work) / len(pieces)
    done, spent = 0, 0.0
    for j, c0 in enumerate(pieces):
        p_ref[:, c0:c0 + PROJ_PIECE] = jnp.dot(hb_ref[...], win_ref[:, c0:c0 + PROJ_PIECE],
                                               preferred_element_type=F32)
        while done < len(work) and spent < (j + 1) * per_piece:
            cost, emit = work[done]
            emit()
            spent += cost
            done += 1

    u = cbuf[...]
    mu = jnp.mean(u, axis=-1, keepdims=True)
    d = u - mu
    var = jnp.mean(d * d, axis=-1, keepdims=True)
    ln = d * lax.rsqrt(var + EPS) * lng_ref[...] + lnb_ref[...]
    o_ref[:, POOL_WIDTH + sc_width:] = (ln * jax.nn.sigmoid(ln)).astype(o_ref.dtype)

    keep = (jnp.minimum(i, n_tiles - 1) % tiles_per_seq) != 0
    vbuf[0:POOL_HIST, :] = jnp.where(keep, vbuf[tm:tm + POOL_HIST, :], 0.0)
    sbuf[0:SC_HIST, :] = jnp.where(keep, sbuf[tm:tm + SC_HIST, :], 0.0)
    ubuf[0:CF_HIST, :] = jnp.where(keep, ubuf[tm:tm + CF_HIST, :], 0.0)
    vbuf[POOL_HIST:POOL_HIST + tm, :] = p_ref[:, 0:POOL_WIDTH]
    bbuf[...] = p_ref[:, o_sc:o_sc + sc_width]
    sbuf[SC_HIST:SC_HIST + tm, :] = p_ref[:, o_c:o_c + sc_width] * p_ref[:, o_h:o_h + sc_width]
    ubuf[CF_HIST:CF_HIST + tm, :] = (p_ref[:, o_a:o_a + cf_width]
                                     * jax.nn.sigmoid(p_ref[:, o_g:o_g + cf_width]))


def _mix_in(x2, mod_l, g, w_in, pool_w, pool_scale, sc_conv, cf_conv, ln_g, ln_b, *, seq):
    n, d = x2.shape
    d_in = w_in.shape[1]
    sc_width = sc_conv.shape[1]
    cf_width = cf_conv.shape[1]
    tm = MIX_ROWS
    n_tiles = n // tm
    tiles_per_seq = seq // tm
    const = lambda i: (0, 0)
    cur = lambda i: jnp.minimum(i, n_tiles - 1)
    return pl.pallas_call(
        functools.partial(_mix_in_kernel, tiles_per_seq=tiles_per_seq, n_tiles=n_tiles, tm=tm,
                          rc=MIX_ROW_CHUNK, sc_width=sc_width, cf_width=cf_width),
        out_shape=jax.ShapeDtypeStruct((n, d), BF16),
        grid=(n_tiles + 1,),
        in_specs=[pl.BlockSpec((tm, d), lambda i: (cur(i), 0)),
                  pl.BlockSpec((1, 6, d), lambda i: (cur(i) // tiles_per_seq, 0, 0)),
                  pl.BlockSpec((1, d), const),
                  pl.BlockSpec((d, d_in), const, pipeline_mode=pl.Buffered(1)),
                  pl.BlockSpec(pool_w.shape, lambda i: (0, 0, 0)),
                  pl.BlockSpec((1, POOL_WIDTH), const),
                  pl.BlockSpec(sc_conv.shape, const),
                  pl.BlockSpec(cf_conv.shape, const),
                  pl.BlockSpec((1, cf_width), const),
                  pl.BlockSpec((1, cf_width), const)],
        out_specs=pl.BlockSpec((tm, d), lambda i: (jnp.maximum(i, 1) - 1, 0)),
        scratch_shapes=[pltpu.VMEM((tm, d_in), F32),
                        pltpu.VMEM((tm, d), BF16),
                        pltpu.VMEM((POOL_HIST + tm, POOL_WIDTH), F32),
                        pltpu.VMEM((SC_HIST + tm, sc_width), F32),
                        pltpu.VMEM((CF_HIST + tm, cf_width), F32),
                        pltpu.VMEM((tm, sc_width), F32),
                        pltpu.VMEM((tm, cf_width), F32)],
        compiler_params=_params(("arbitrary",), 48),
        name="mix_in",
    )(x2, mod_l, g.reshape(1, d), w_in, pool_w, pool_scale.reshape(1, -1), sc_conv, cf_conv,
      ln_g.reshape(1, -1), ln_b.reshape(1, -1))


def _mix_out_kernel(m_ref, w_ref, x_ref, mod_ref, g_ref, o_ref):
    y = jnp.dot(m_ref[...], w_ref[...], preferred_element_type=F32)
    o_ref[...] = x_ref[...] + mod_ref[0, 2:3, :] * _rms(y, g_ref[...])


def _mix_out(mixed, w_out, x2, mod_l, g, *, seq):
    n, d = x2.shape
    tm = 512
    tiles_per_seq = seq // tm
    return pl.pallas_call(
        _mix_out_kernel,
        out_shape=jax.ShapeDtypeStruct((n, d), F32),
        grid=(n // tm,),
        in_specs=[pl.BlockSpec((tm, d), lambda i: (i, 0)),
                  pl.BlockSpec((d, d), lambda i: (0, 0), pipeline_mode=pl.Buffered(1)),
                  pl.BlockSpec((tm, d), lambda i: (i, 0)),
                  pl.BlockSpec((1, 6, d), lambda i: (i // tiles_per_seq, 0, 0)),
                  pl.BlockSpec((1, d), lambda i: (0, 0))],
        out_specs=pl.BlockSpec((tm, d), lambda i: (i, 0)),
        compiler_params=_params(("arbitrary",), 48),
        name="mix_out",
    )(mixed, w_out, x2, mod_l, g.reshape(1, d))


def _swiglu_accumulate(h, w1_ref, w3_ref, w2_ref, o_ref):
    a1 = jnp.dot(h, w1_ref[...], preferred_element_type=F32)
    a3 = jnp.dot(h, w3_ref[...], preferred_element_type=F32)
    act = (a1 * jax.nn.sigmoid(a1) * a3).astype(BF16)
    o_ref[...] += jnp.dot(act, w2_ref[...], preferred_element_type=F32)


def _dense_ffn_kernel(*refs, nf, n_cast):
    x_ref, mod_ref, gpre_ref, gpost_ref, w1_ref, w3_ref, w2_ref = refs[:7]
    cast_in = refs[7:7 + n_cast]
    o_ref = refs[7 + n_cast]
    cast_out = refs[8 + n_cast:8 + 2 * n_cast]
    h_scr = refs[8 + 2 * n_cast]
    f = pl.program_id(1)

    @pl.when(f == 0)
    def _():
        h = _modulated_rms(x_ref[...], gpre_ref[...], mod_ref[0, 4:5, :], mod_ref[0, 3:4, :])
        h_scr[...] = h.astype(BF16)
        o_ref[...] = jnp.zeros_like(o_ref)

    _swiglu_accumulate(h_scr[...], w1_ref, w3_ref, w2_ref, o_ref)
    for src, dst in zip(cast_in, cast_out):
        dst[...] = src[...].astype(BF16)

    @pl.when(f == nf - 1)
    def _():
        o_ref[...] = x_ref[...] + mod_ref[0, 5:6, :] * _rms(o_ref[...], gpost_ref[...])


def _dense_ffn(x2, mod_l, g_pre, g_post, w1, w3, w2, cast_weights, *, seq):
    n, d = x2.shape
    dff = w1.shape[1]
    tm, tf = DENSE_FFN_ROWS, FFN_COLS
    nf = dff // tf
    n_i = n // tm
    tiles_per_seq = seq // tm
    assert nf >= CAST_SPLIT or not cast_weights
    views, cast_specs = [], []
    for w in cast_weights:
        v = w.reshape(-1, w.shape[-1])
        rows, cols = v.shape
        assert rows % (n_i * 16) == 0 and cols % (CAST_SPLIT * LANE) == 0, v.shape
        views.append(v)
        cast_specs.append(pl.BlockSpec((rows // n_i, cols // CAST_SPLIT),
                                       lambda i, f: (i, jnp.minimum(f, CAST_SPLIT - 1))))
    outs = pl.pallas_call(
        functools.partial(_dense_ffn_kernel, nf=nf, n_cast=len(views)),
        out_shape=[jax.ShapeDtypeStruct((n, d), F32)]
                  + [jax.ShapeDtypeStruct(v.shape, BF16) for v in views],
        grid=(n_i, nf),
        in_specs=[pl.BlockSpec((tm, d), lambda i, f: (i, 0)),
                  pl.BlockSpec((1, 6, d), lambda i, f: (i // tiles_per_seq, 0, 0)),
                  pl.BlockSpec((1, d), lambda i, f: (0, 0)),
                  pl.BlockSpec((1, d), lambda i, f: (0, 0)),
                  pl.BlockSpec((d, tf), lambda i, f: (0, f)),
                  pl.BlockSpec((d, tf), lambda i, f: (0, f)),
                  pl.BlockSpec((tf, d), lambda i, f: (f, 0))] + cast_specs,
        out_specs=[pl.BlockSpec((tm, d), lambda i, f: (i, 0))] + cast_specs,
        scratch_shapes=[pltpu.VMEM((tm, d), BF16)],
        compiler_params=_params(("arbitrary", "arbitrary"), 56),
        name="dense_swiglu",
    )(x2, mod_l, g_pre.reshape(1, d), g_post.reshape(1, d), w1, w3, w2, *views)
    return outs[0], [o.reshape(w.shape) for o, w in zip(outs[1:], cast_weights)]


def _expert_ffn_kernel(te_ref, nv_ref, hs_ref, w1_ref, w3_ref, w2_ref, o_ref):
    i = pl.program_id(0)
    f = pl.program_id(1)

    @pl.when(f == 0)
    def _():
        o_ref[...] = jnp.zeros_like(o_ref)

    @pl.when(i < nv_ref[0])
    def _():
        _swiglu_accumulate(hs_ref[...].astype(BF16), w1_ref, w3_ref, w2_ref, o_ref)


def _expert_ffn(tile_expert, n_valid, hs, w1, w3, w2, *, tm):
    p, d = hs.shape
    dff = w1.shape[2]
    tf = EXPERT_COLS
    nf = dff // tf
    n_tiles = p // tm

    def row_map(i, f, te, nv):
        return (i, 0)

    def f_of(i, f, nv):
        return jnp.where(i < nv[0], f, nf - 1)

    grid_spec = pltpu.PrefetchScalarGridSpec(
        num_scalar_prefetch=2,
        grid=(n_tiles, nf),
        in_specs=[pl.BlockSpec((tm, d), row_map),
                  pl.BlockSpec((None, d, tf), lambda i, f, te, nv: (te[i], 0, f_of(i, f, nv))),
                  pl.BlockSpec((None, d, tf), lambda i, f, te, nv: (te[i], 0, f_of(i, f, nv))),
                  pl.BlockSpec((None, tf, d), lambda i, f, te, nv: (te[i], f_of(i, f, nv), 0))],
        out_specs=pl.BlockSpec((tm, d), row_map),
    )
    return pl.pallas_call(
        _expert_ffn_kernel,
        out_shape=jax.ShapeDtypeStruct((p, d), F32),
        grid_spec=grid_spec,
        compiler_params=_params(("arbitrary", "arbitrary"), 56),
        name="expert_swiglu",
    )(tile_expert, n_valid, hs, w1, w3, w2)


def _router_kernel(x_ref, mod_ref, g_ref, rw_ref, rb_ref, h_ref, route_ref, cnt_ref, carry, *, tm):
    i = pl.program_id(0)

    @pl.when(i == 0)
    def _():
        carry[...] = jnp.zeros_like(carry)

    h = _modulated_rms(x_ref[...], g_ref[...], mod_ref[0, 4:5, :], mod_ref[0, 3:4, :])
    h_ref[...] = h
    logits = jnp.dot(h.astype(BF16), rw_ref[...], preferred_element_type=F32) + rb_ref[...]
    lane = lax.broadcasted_iota(jnp.int32, (tm, LANE), 1)
    m1 = jnp.max(logits, axis=-1, keepdims=True)
    i1 = jnp.min(jnp.where(logits == m1, lane, LANE), axis=-1, keepdims=True)
    rest = jnp.where(lane == i1, -jnp.inf, logits)
    m2 = jnp.max(rest, axis=-1, keepdims=True)
    i2 = jnp.min(jnp.where(rest == m2, lane, LANE), axis=-1, keepdims=True)
    e = jnp.exp(m2 - m1)
    p1 = 1.0 / (1.0 + e)
    p2 = e / (1.0 + e)

    sel1 = lane == i1
    sel2 = lane == i2
    sel = jnp.logical_or(sel1, sel2).astype(F32)
    rows = lax.broadcasted_iota(jnp.int32, (tm, tm), 0)
    cols = lax.broadcasted_iota(jnp.int32, (tm, tm), 1)
    lower = (rows > cols).astype(BF16)
    rank = jnp.dot(lower, sel.astype(BF16), preferred_element_type=F32) + carry[...]
    r1 = jnp.sum(jnp.where(sel1, rank, 0.0), axis=-1, keepdims=True)
    r2 = jnp.sum(jnp.where(sel2, rank, 0.0), axis=-1, keepdims=True)
    carry[...] += jnp.sum(sel, axis=0, keepdims=True)
    cnt_ref[...] = carry[...]

    slab = jnp.where(lane == 0, i1.astype(F32), 0.0)
    slab = jnp.where(lane == 1, i2.astype(F32), slab)
    slab = jnp.where(lane == 2, r1, slab)
    slab = jnp.where(lane == 3, r2, slab)
    slab = jnp.where(lane == 4, p1, slab)
    slab = jnp.where(lane == 5, p2, slab)
    route_ref[...] = slab


def _router(x2, mod_l, g, rw_pad, rb_pad, *, seq):
    n, d = x2.shape
    tm = 512
    tiles_per_seq = seq // tm
    return pl.pallas_call(
        functools.partial(_router_kernel, tm=tm),
        out_shape=(jax.ShapeDtypeStruct((n, d), F32),
                   jax.ShapeDtypeStruct((n, LANE), F32),
                   jax.ShapeDtypeStruct((1, LANE), F32)),
        grid=(n // tm,),
        in_specs=[pl.BlockSpec((tm, d), lambda i: (i, 0)),
                  pl.BlockSpec((1, 6, d), lambda i: (i // tiles_per_seq, 0, 0)),
                  pl.BlockSpec((1, d), lambda i: (0, 0)),
                  pl.BlockSpec((d, LANE), lambda i: (0, 0)),
                  pl.BlockSpec((1, LANE), lambda i: (0, 0))],
        out_specs=(pl.BlockSpec((tm, d), lambda i: (i, 0)),
                   pl.BlockSpec((tm, LANE), lambda i: (i, 0)),
                   pl.BlockSpec((1, LANE), lambda i: (0, 0))),
        scratch_shapes=[pltpu.VMEM((1, LANE), F32)],
        compiler_params=_params(("arbitrary",), 40),
        name="router",
    )(x2, mod_l, g.reshape(1, d), rw_pad, rb_pad)


def _dispatch_kernel(pad_start_ref, pad_cnt_ref, pos_ref, h_ref, hs_ref, zero_scr, sem, zsem, *, tm):
    i = pl.program_id(0)

    def row_copy(r, k):
        return pltpu.make_async_copy(h_ref.at[pl.ds(r, 1), :],
                                     hs_ref.at[pl.ds(pos_ref[TOP_K * r + k], 1), :], sem)

    def issue(r, carry):
        for k in range(TOP_K):
            row_copy(r, k).start(priority=k)
        return carry
    lax.fori_loop(0, tm, issue, 0, unroll=ROW_DMA_UNROLL)

    @pl.when(i == 0)
    def _():
        zero_scr[...] = jnp.zeros_like(zero_scr)
        for e in range(N_EXPERTS + 1):
            def zcopy(j, e=e):
                return pltpu.make_async_copy(zero_scr.at[pl.ds(0, 1), :],
                                             hs_ref.at[pl.ds(pad_start_ref[e] + j, 1), :], zsem)

            def zissue(j, carry):
                zcopy(j).start()
                return carry

            def zwait(j, carry):
                zcopy(j).wait()
                return carry
            lax.fori_loop(0, pad_cnt_ref[e], zissue, 0)
            lax.fori_loop(0, pad_cnt_ref[e], zwait, 0)

    for k in range(TOP_K):
        pltpu.make_async_copy(h_ref, hs_ref.at[pl.ds(0, tm), :], sem).wait()


def _dispatch(pad_start, pad_cnt, pos_flat, h2, *, p_rows):
    n, d = h2.shape
    tm = 512
    grid_spec = pltpu.PrefetchScalarGridSpec(
        num_scalar_prefetch=2,
        grid=(n // tm,),
        in_specs=[pl.BlockSpec((TOP_K * tm,), lambda i, ps, pc: (i,), memory_space=pltpu.SMEM),
                  pl.BlockSpec((tm, d), lambda i, ps, pc: (i, 0))],
        out_specs=pl.BlockSpec(memory_space=pl.ANY),
        scratch_shapes=[pltpu.VMEM((8, d), F32),
                        pltpu.SemaphoreType.DMA(()),
                        pltpu.SemaphoreType.DMA(())],
    )
    return pl.pallas_call(
        functools.partial(_dispatch_kernel, tm=tm),
        out_shape=jax.ShapeDtypeStruct((p_rows, d), F32),
        grid_spec=grid_spec,
        compiler_params=_params(("arbitrary",), 32),
        name="dispatch_rows",
    )(pad_start, pad_cnt, pos_flat, h2)


def _combine_kernel(pos_ref, os_ref, x_ref, route_ref, mod_ref, g_ref, o_ref, a_scr, b_scr, sem, *, tm):
    def row_copy(r, k):
        dst = a_scr if k == 0 else b_scr
        return pltpu.make_async_copy(os_ref.at[pl.ds(pos_ref[TOP_K * r + k], 1), :],
                                     dst.at[pl.ds(r, 1), :], sem)

    def issue(r, carry):
        for k in range(TOP_K):
            row_copy(r, k).start(priority=k)
        return carry
    lax.fori_loop(0, tm, issue, 0, unroll=ROW_DMA_UNROLL)

    for dst in (a_scr, b_scr):
        pltpu.make_async_copy(os_ref.at[pl.ds(0, tm), :], dst, sem).wait()

    y = route_ref[:, 4:5] * a_scr[...] + route_ref[:, 5:6] * b_scr[...]
    o_ref[...] = x_ref[...] + mod_ref[0, 5:6, :] * _rms(y, g_ref[...])


def _combine(pos_flat, o_sorted, x2, route, mod_l, g, *, seq):
    n, d = x2.shape
    tm = 512
    tiles_per_seq = seq // tm
    return pl.pallas_call(
        functools.partial(_combine_kernel, tm=tm),
        out_shape=jax.ShapeDtypeStruct((n, d), F32),
        grid=(n // tm,),
        in_specs=[pl.BlockSpec((TOP_K * tm,), lambda i: (i,), memory_space=pltpu.SMEM),
                  pl.BlockSpec(memory_space=pl.ANY),
                  pl.BlockSpec((tm, d), lambda i: (i, 0)),
                  pl.BlockSpec((tm, LANE), lambda i: (i, 0)),
                  pl.BlockSpec((1, 6, d), lambda i: (i // tiles_per_seq, 0, 0)),
                  pl.BlockSpec((1, d), lambda i: (0, 0))],
        out_specs=pl.BlockSpec((tm, d), lambda i: (i, 0)),
        scratch_shapes=[pltpu.VMEM((tm, d), F32),
                        pltpu.VMEM((tm, d), F32),
                        pltpu.SemaphoreType.DMA(())],
        compiler_params=_params(("arbitrary",), 40),
        name="combine_rows",
    )(pos_flat, o_sorted, x2, route, mod_l, g.reshape(1, d))


def _moe_ffn(x2, mod_l, g_pre, g_post, router_w, router_b, w1, w3, w2, *, seq):
    n, d = x2.shape
    tm = EXPERT_ROWS
    n_tiles = (TOP_K * n) // tm + N_EXPERTS
    rw_pad = jnp.pad(router_w, ((0, 0), (0, LANE - N_EXPERTS))).astype(BF16)
    rb_pad = jnp.pad(router_b.reshape(1, -1), ((0, 0), (0, LANE - N_EXPERTS)),
                     constant_values=NEG_BIG)
    h2, route, cnt = _router(x2, mod_l, g_pre, rw_pad, rb_pad, seq=seq)

    counts = cnt[0, :N_EXPERTS].astype(jnp.int32)
    tiles_e = (counts + tm - 1) // tm
    tile_end = jnp.cumsum(tiles_e)
    group_start = (tile_end - tiles_e) * tm
    n_valid = tile_end[-1:]
    tile_ids = jnp.minimum(jnp.arange(n_tiles, dtype=jnp.int32), n_valid[0] - 1)
    tile_expert = jnp.sum(tile_ids[:, None] >= tile_end[None, :], axis=1).astype(jnp.int32)
    e_idx = route[:, 0:TOP_K].astype(jnp.int32)
    rank = route[:, TOP_K:2 * TOP_K].astype(jnp.int32)
    pos_flat = (group_start[e_idx] + rank).reshape(-1)

    pad_start = jnp.concatenate([group_start + counts, n_valid * tm])
    pad_cnt = jnp.concatenate([tiles_e * tm - counts, (n_tiles - n_valid) * tm])
    hs = _dispatch(pad_start, pad_cnt, pos_flat, h2, p_rows=n_tiles * tm)
    o_sorted = _expert_ffn(tile_expert, n_valid, hs, w1, w3, w2, tm=tm)
    return _combine(pos_flat, o_sorted, x2, route, mod_l, g_post, seq=seq)


def kernel(x, c, mod_w, mod_b, pre_mix_g, post_mix_g, pre_ffn_g, post_ffn_g, w_in, w_out, pool_w,
           pool_scale, sc_conv, cf_conv, cf_ln_g, cf_ln_b, ffn_w1, ffn_w3, ffn_w2, router_w,
           router_b, moe_w1, moe_w3, moe_w2):
    b, s, d = x.shape
    depth = mod_w.shape[0]
    mod = _modulation(c, mod_w, mod_b)
    x2 = x.reshape(b * s, d)
    moe_bf16 = {}
    for l in range(depth):
        mixed = _mix_in(x2, mod[l], pre_mix_g[l], w_in[l].astype(BF16), pool_w[l].astype(BF16),
                        pool_scale[l], sc_conv[l], cf_conv[l], cf_ln_g[l], cf_ln_b[l], seq=s)
        x2 = _mix_out(mixed, w_out[l].astype(BF16), x2, mod[l], post_mix_g[l], seq=s)
        j = l // 2
        if l % 2 == 0:
            nxt = [moe_w1[j], moe_w3[j], moe_w2[j]] if l + 1 < depth else []
            x2, cast = _dense_ffn(x2, mod[l], pre_ffn_g[l], post_ffn_g[l], ffn_w1[j].astype(BF16),
                                  ffn_w3[j].astype(BF16), ffn_w2[j].astype(BF16), nxt, seq=s)
            if cast:
                moe_bf16[j] = cast
        else:
            w1b, w3b, w2b = moe_bf16.get(j) or [w[j].astype(BF16) for w in (moe_w1, moe_w3, moe_w2)]
            x2 = _moe_ffn(x2, mod[l], pre_ffn_g[l], post_ffn_g[l], router_w[j], router_b[j],
                          w1b, w3b, w2b, seq=s)
    return x2.reshape(b, s, d)
```
